```python
import jax, jax.numpy as jnp
from jax import lax
import numpy as np

D_MODEL = 1024
BATCH = 16
SEQ = 2048
DEPTH = 2
DEC_BATCH = 32
DEC_SEQ = 32
PAST_LEN = 2048

CHUNK = 64
MIX_WIDTH = D_MODEL
POOL_WIDTH = MIX_WIDTH // 2
CONV_WIDTH = MIX_WIDTH - POOL_WIDTH
POOL_WINDOWS = (2, 4, 8, 16)
N_POOL_GROUPS = len(POOL_WINDOWS)
POOL_GROUP = POOL_WIDTH // N_POOL_GROUPS
POOL_HIST = max(POOL_WINDOWS) - 1
CONV_K = 3
IN_WIDTH = POOL_WIDTH + 3 * CONV_WIDTH
N_MEM = 256
N_MEM_HEADS = 4
MEM_HEAD_DIM = D_MODEL // N_MEM_HEADS
D_FF = 2816
FFN_CONV_K = 3
EPS = 1e-6

kernel_name = "hybrid_pool_shortconv_stream_step"


def rms_norm(x, g):
    x32 = x.astype(jnp.float32)
    y = x32 * lax.rsqrt(jnp.mean(x32 * x32, axis=-1, keepdims=True) + EPS)
    return (y * g.astype(jnp.float32)).astype(x.dtype)


def causal_dwconv3(z, hist, w, b):
    t = z.shape[1]
    ext = jnp.concatenate([hist.astype(z.dtype), z], axis=1)
    y = ext[:, 0:t] * w[0] + ext[:, 1:t + 1] * w[1] + ext[:, 2:t + 2] * w[2] + b
    return y, ext[:, -2:]


def multiscale_pool(u, hist, start_pos, w_pool, scale):
    b, t, c = u.shape
    ext = jnp.concatenate([hist.astype(u.dtype), u], axis=1)
    cs = jnp.cumsum(ext.astype(jnp.float32), axis=1)
    cs = jnp.pad(cs, ((0, 0), (1, 0), (0, 0)))
    pos = start_pos + jnp.arange(t)
    p1 = POOL_HIST + 1
    means = []
    for g, w in enumerate(POOL_WINDOWS):
        lo, hi = g * POOL_GROUP, (g + 1) * POOL_GROUP
        s = cs[:, p1:p1 + t, lo:hi] - cs[:, p1 - w:p1 - w + t, lo:hi]
        cnt = jnp.minimum(w, pos + 1).astype(jnp.float32)[None, :, None]
        means.append(s / cnt)
    mean = jnp.concatenate(means, axis=-1)
    d = (mean - u.astype(jnp.float32)).astype(u.dtype)
    d = d.reshape(b, t, N_POOL_GROUPS, POOL_GROUP)
    y = jnp.einsum('btgc,gcd->btgd', d, w_pool).reshape(b, t, c) * scale
    return y, ext[:, -POOL_HIST:]


def memory_kv(mem, g_mem, w_k, w_v):
    m = rms_norm(mem, g_mem)
    k = jnp.einsum('bmd,dhe->bmhe', m, w_k)
    v = jnp.einsum('bmd,dhe->bmhe', m, w_v)
    return k, v


def memory_attention(h, k, v, w_q, w_o):
    q = jnp.einsum('btd,dhe->bthe', h, w_q)
    s = jnp.einsum('bthe,bmhe->bhtm', q, k).astype(jnp.float32) * (MEM_HEAD_DIM ** -0.5)
    p = jax.nn.softmax(s, axis=-1).astype(v.dtype)
    o = jnp.einsum('bhtm,bmhe->bthe', p, v)
    return jnp.einsum('bthe,hed->btd', o, w_o)


def layer_step(x, mem_k, mem_v, pool_hist, conv_hist, ffn_hist, start_pos, lw):
    h = rms_norm(x, lw['g_mix_pre'])
    proj = h @ lw['w_in']
    u_a = proj[..., :POOL_WIDTH]
    b_gate, c_gate, val = jnp.split(proj[..., POOL_WIDTH:], 3, axis=-1)
    y_a, pool_new = multiscale_pool(u_a, pool_hist, start_pos, lw['w_pool'], lw['pool_scale'])
    zc, conv_new = causal_dwconv3(c_gate * val, conv_hist, lw['conv_w'], lw['conv_b'])
    y_b = b_gate * zc
    y = jnp.concatenate([y_a, y_b], axis=-1) @ lw['w_out']
    x = x + rms_norm(y, lw['g_mix_post'])
    h = rms_norm(x, lw['g_attn_pre'])
    y = memory_attention(h, mem_k, mem_v, lw['w_q'], lw['w_o'])
    x = x + rms_norm(y, lw['g_attn_post'])
    h = rms_norm(x, lw['g_ffn_pre'])
    up = h @ lw['w_up']
    upc, ffn_new = causal_dwconv3(up, ffn_hist, lw['ffn_conv_w'], lw['ffn_conv_b'])
    gate, value = jnp.split(upc, 2, axis=-1)
    y = (jax.nn.silu(gate) * value) @ lw['w_down']
    x = x + rms_norm(y, lw['g_ffn_post'])
    return x, pool_new, conv_new, ffn_new


def setup_inputs(seed: int = 0) -> dict:
    key = jax.random.key(seed)
    ks = jax.random.split(key, 32)
    f32 = jnp.float32
    L = DEPTH

    def nrm(k, shape, scale):
        return jax.random.normal(k, shape, f32) * scale

    def gain(k, shape):
        return 1.0 + 0.05 * jax.random.normal(k, shape, f32)

    return {
        'x_prompt': nrm(ks[0], (BATCH, SEQ, D_MODEL), 1.0),
        'x_sample': nrm(ks[1], (DEC_BATCH, DEC_SEQ, D_MODEL), 1.0),
        'mem_prompt': nrm(ks[2], (BATCH, N_MEM, D_MODEL), 1.0),
        'cache_mem_k': nrm(ks[3], (L, DEC_BATCH, N_MEM, N_MEM_HEADS, MEM_HEAD_DIM), 1.0),
        'cache_mem_v': nrm(ks[4], (L, DEC_BATCH, N_MEM, N_MEM_HEADS, MEM_HEAD_DIM), 1.0),
        'state_pool': nrm(ks[5], (L, DEC_BATCH, POOL_HIST, POOL_WIDTH), 1.0),
        'state_conv': nrm(ks[6], (L, DEC_BATCH, CONV_K - 1, CONV_WIDTH), 1.0),
        'state_ffn_conv': nrm(ks[7], (L, DEC_BATCH, FFN_CONV_K - 1, 2 * D_FF), 1.0),
        'g_mix_pre': gain(ks[8], (L, D_MODEL)),
        'g_mix_post': gain(ks[9], (L, D_MODEL)),
        'w_in': nrm(ks[10], (L, D_MODEL, IN_WIDTH), D_MODEL ** -0.5),
        'w_pool': nrm(ks[11], (L, N_POOL_GROUPS, POOL_GROUP, POOL_GROUP), POOL_GROUP ** -0.5),
        'pool_scale': gain(ks[12], (L, POOL_WIDTH)),
        'conv_w': nrm(ks[13], (L, CONV_K, CONV_WIDTH), CONV_K ** -0.5),
        'conv_b': nrm(ks[14], (L, CONV_WIDTH), 0.01),
        'w_out': nrm(ks[15], (L, MIX_WIDTH, D_MODEL), MIX_WIDTH ** -0.5),
        'g_attn_pre': gain(ks[16], (L, D_MODEL)),
        'g_attn_post': gain(ks[17], (L, D_MODEL)),
        'g_mem': gain(ks[18], (L, D_MODEL)),
        'w_q': nrm(ks[19], (L, D_MODEL, N_MEM_HEADS, MEM_HEAD_DIM), D_MODEL ** -0.5),
        'w_k': nrm(ks[20], (L, D_MODEL, N_MEM_HEADS, MEM_HEAD_DIM), D_MODEL ** -0.5),
        'w_v': nrm(ks[21], (L, D_MODEL, N_MEM_HEADS, MEM_HEAD_DIM), D_MODEL ** -0.5),
        'w_o': nrm(ks[22], (L, N_MEM_HEADS, MEM_HEAD_DIM, D_MODEL), D_MODEL ** -0.5),
        'g_ffn_pre': gain(ks[23], (L, D_MODEL)),
        'g_ffn_post': gain(ks[24], (L, D_MODEL)),
        'w_up': nrm(ks[25], (L, D_MODEL, 2 * D_FF), D_MODEL ** -0.5),
        'ffn_conv_w': nrm(ks[26], (L, FFN_CONV_K, 2 * D_FF), FFN_CONV_K ** -0.5),
        'ffn_conv_b': nrm(ks[27], (L, 2 * D_FF), 0.01),
        'w_down': nrm(ks[28], (L, D_FF, D_MODEL), D_FF ** -0.5),
    }


def reference(x_prompt, x_sample, mem_prompt, cache_mem_k, cache_mem_v, state_pool, state_conv,
              state_ffn_conv, g_mix_pre, g_mix_post, w_in, w_pool, pool_scale, conv_w, conv_b, w_out,
              g_attn_pre, g_attn_post, g_mem, w_q, w_k, w_v, w_o, g_ffn_pre, g_ffn_post, w_up,
              ffn_conv_w, ffn_conv_b, w_down):
    yp, ys = x_prompt, x_sample
    bp = x_prompt.shape[0]
    mk_p, mv_p, pool_p, conv_p, ffn_p = [], [], [], [], []
    pool_s, conv_s, ffn_s = [], [], []
    for l in range(DEPTH):
        lw = {
            'g_mix_pre': g_mix_pre[l], 'g_mix_post': g_mix_post[l], 'w_in': w_in[l],
            'w_pool': w_pool[l], 'pool_scale': pool_scale[l], 'conv_w': conv_w[l], 'conv_b': conv_b[l],
            'w_out': w_out[l], 'g_attn_pre': g_attn_pre[l], 'g_attn_post': g_attn_post[l],
            'w_q': w_q[l], 'w_o': w_o[l], 'g_ffn_pre': g_ffn_pre[l], 'g_ffn_post': g_ffn_post[l],
            'w_up': w_up[l], 'ffn_conv_w': ffn_conv_w[l], 'ffn_conv_b': ffn_conv_b[l], 'w_down': w_down[l],
        }
        kp, vp = memory_kv(mem_prompt, g_mem[l], w_k[l], w_v[l])
        yp, pn, cn, fn = layer_step(
            yp, kp, vp,
            jnp.zeros((bp, POOL_HIST, POOL_WIDTH), yp.dtype),
            jnp.zeros((bp, CONV_K - 1, CONV_WIDTH), yp.dtype),
            jnp.zeros((bp, FFN_CONV_K - 1, 2 * D_FF), yp.dtype),
            0, lw)
        mk_p.append(kp)
        mv_p.append(vp)
        pool_p.append(pn)
        conv_p.append(cn)
        ffn_p.append(fn)
        ys, pn, cn, fn = layer_step(ys, cache_mem_k[l], cache_mem_v[l], state_pool[l], state_conv[l],
                                    state_ffn_conv[l], PAST_LEN, lw)
        pool_s.append(pn)
        conv_s.append(cn)
        ffn_s.append(fn)
    return (yp, ys, jnp.stack(mk_p), jnp.stack(mv_p), jnp.stack(pool_p), jnp.stack(conv_p),
            jnp.stack(ffn_p), jnp.stack(pool_s), jnp.stack(conv_s), jnp.stack(ffn_s))
```

```python
import functools

import jax
import jax.numpy as jnp
from jax import lax
from jax.experimental import pallas as pl
from jax.experimental.pallas import tpu as pltpu

D_MODEL = 1024
POOL_WIDTH = 512
CONV_WIDTH = 512
POOL_WINDOWS = (2, 4, 8, 16)
POOL_GROUP = 128
POOL_HIST = 15
POOL_PAD = 16
HIST_PAD = 8
IN_WIDTH = 2048
N_MEM = 256
N_HEADS = 4
HEAD_DIM = 256
D_FF = 2816
FFN_CHUNK = 256
EPS = 1e-6
PAST_LEN = 2048

VMEM_LIMIT_BYTES = 56 * 1024 * 1024

_BF16 = jnp.bfloat16
_F32 = jnp.float32


def _rms(x, g):
    ms = jnp.mean(x * x, axis=-1, keepdims=True)
    return x * lax.rsqrt(ms + EPS) * g


def _dot(a, b):
    return jnp.dot(a, b, preferred_element_type=_F32)


def _shift_rows(ext3, k, pad):
    nb, rows, c = ext3.shape
    flat = ext3.reshape(nb * rows, c)
    return pltpu.roll(flat, k, 0).reshape(nb, rows, c)[:, pad:, :]


def _const_spec(shape, layer):
    nd = len(shape)
    return pl.BlockSpec((None,) + tuple(shape[1:]), lambda b, s: (layer,) + (0,) * (nd - 1),
                        pipeline_mode=pl.Buffered(1))


def _params():
    return pltpu.CompilerParams(dimension_semantics=("arbitrary", "arbitrary"),
                                vmem_limit_bytes=VMEM_LIMIT_BYTES)


def _mixer_kernel(x_ref, ph_ref, ch_ref, gpre_ref, win_ref, wpool_ref, pscale_ref, cw_ref, cb_ref,
                  wout_ref, gpost_ref, y_ref, pnew_ref, cnew_ref, pcarry, ccarry, *, start_pos):
    nb, t, d = x_ref.shape
    r = nb * t
    s = pl.program_id(1)

    @pl.when(s == 0)
    def _():
        pcarry[...] = ph_ref[...]
        ccarry[...] = ch_ref[...]

    x = x_ref[...].reshape(r, d)
    h = _rms(x, gpre_ref[...]).astype(_BF16)
    proj = _dot(h, win_ref[...])
    u = proj[:, :POOL_WIDTH]
    b_gate = proj[:, POOL_WIDTH:POOL_WIDTH + CONV_WIDTH]
    c_gate = proj[:, POOL_WIDTH + CONV_WIDTH:POOL_WIDTH + 2 * CONV_WIDTH]
    val = proj[:, POOL_WIDTH + 2 * CONV_WIDTH:]

    ext = jnp.concatenate([pcarry[...], u.reshape(nb, t, POOL_WIDTH)], axis=1)
    pos = start_pos + s * t + lax.broadcasted_iota(jnp.int32, (nb, t, 1), 1)
    ya = []
    for g, w in enumerate(POOL_WINDOWS):
        lo = g * POOL_GROUP
        acc = ext[:, :, lo:lo + POOL_GROUP]
        step = 1
        while step < w:
            rolled = pltpu.roll(acc.reshape(nb * (POOL_PAD + t), POOL_GROUP), step, 0)
            acc = acc + rolled.reshape(nb, POOL_PAD + t, POOL_GROUP)
            step *= 2
        cnt = jnp.minimum(w, pos + 1).astype(_F32)
        mean = acc[:, POOL_PAD:, :] / cnt
        dlt = (mean.reshape(r, POOL_GROUP) - u[:, lo:lo + POOL_GROUP]).astype(_BF16)
        ya.append(_dot(dlt, wpool_ref[g]) * pscale_ref[:, lo:lo + POOL_GROUP])
    new_pool = ext[:, t:, :]
    pcarry[...] = new_pool
    pnew_ref[...] = new_pool

    cv = c_gate * val
    extc = jnp.concatenate([ccarry[...], cv.reshape(nb, t, CONV_WIDTH)], axis=1)
    r1 = _shift_rows(extc, 1, HIST_PAD).reshape(r, CONV_WIDTH)
    r2 = _shift_rows(extc, 2, HIST_PAD).reshape(r, CONV_WIDTH)
    zc = r2 * cw_ref[0:1, :] + r1 * cw_ref[1:2, :] + cv * cw_ref[2:3, :] + cb_ref[...]
    yb = b_gate * zc
    new_conv = extc[:, t:, :]
    ccarry[...] = new_conv
    cnew_ref[...] = new_conv

    ycat = jnp.concatenate(ya + [yb], axis=-1).astype(_BF16)
    y = _dot(ycat, wout_ref[...])
    y_ref[...] = (x + _rms(y, gpost_ref[...])).reshape(nb, t, d)


def _mixer_call(x, pool_hist, conv_hist, hist_layer, layer, w, *, nb, t, start_pos):
    b, seq, d = x.shape
    grid = (b // nb, seq // t)
    tile = pl.BlockSpec((nb, t, d), lambda i, s: (i, s, 0))
    phs = pl.BlockSpec((None, nb, POOL_PAD, POOL_WIDTH), lambda i, s: (hist_layer, i, 0, 0))
    chs = pl.BlockSpec((None, nb, HIST_PAD, CONV_WIDTH), lambda i, s: (hist_layer, i, 0, 0))
    pos_ = pl.BlockSpec((nb, POOL_PAD, POOL_WIDTH), lambda i, s: (i, 0, 0))
    cos_ = pl.BlockSpec((nb, HIST_PAD, CONV_WIDTH), lambda i, s: (i, 0, 0))
    names = ('g_mix_pre', 'w_in', 'w_pool', 'pool_scale', 'conv_w', 'conv_b', 'w_out', 'g_mix_post')
    return pl.pallas_call(
        functools.partial(_mixer_kernel, start_pos=start_pos),
        grid=grid,
        in_specs=[tile, phs, chs] + [_const_spec(w[n].shape, layer) for n in names],
        out_specs=[tile, pos_, cos_],
        out_shape=[jax.ShapeDtypeStruct(x.shape, _F32),
                   jax.ShapeDtypeStruct((b, POOL_PAD, POOL_WIDTH), _F32),
                   jax.ShapeDtypeStruct((b, HIST_PAD, CONV_WIDTH), _F32)],
        scratch_shapes=[pltpu.VMEM((nb, POOL_PAD, POOL_WIDTH), _F32),
                        pltpu.VMEM((nb, HIST_PAD, CONV_WIDTH), _F32)],
        compiler_params=_params(),
        name="mixer",
    )(x, pool_hist, conv_hist, *[w[n] for n in names])


def _attn_kernel(x_ref, k_ref, v_ref, gpre_ref, wq_ref, wo_ref, gpost_ref, y_ref, k_sc, v_sc):
    nb, t, d = x_ref.shape
    r = nb * t

    @pl.when(pl.program_id(1) == 0)
    def _():
        k_sc[...] = k_ref[...].astype(_BF16)
        v_sc[...] = v_ref[...].astype(_BF16)

    x = x_ref[...].reshape(r, d)
    h = _rms(x, gpre_ref[...]).astype(_BF16)
    q = _dot(h, wq_ref[...]).astype(_BF16)
    rows = []
    for n in range(nb):
        heads = []
        for hd in range(N_HEADS):
            cols = slice(hd * HEAD_DIM, (hd + 1) * HEAD_DIM)
            qh = q[n * t:(n + 1) * t, cols]
            sc = lax.dot_general(qh, k_sc[n, :, cols], (((1,), (1,)), ((), ())),
                                 preferred_element_type=_F32)
            e = jnp.exp(sc - jnp.max(sc, axis=-1, keepdims=True))
            p = (e / jnp.sum(e, axis=-1, keepdims=True)).astype(_BF16)
            heads.append(_dot(p, v_sc[n, :, cols]))
        rows.append(jnp.concatenate(heads, axis=-1))
    o = (rows[0] if nb == 1 else jnp.concatenate(rows, axis=0)).astype(_BF16)
    y = _dot(o, wo_ref[...])
    y_ref[...] = (x + _rms(y, gpost_ref[...])).reshape(nb, t, d)


def _attn_call(x, mem_k, mem_v, kv_layer, layer, w, *, nb, t):
    b, seq, d = x.shape
    grid = (b // nb, seq // t)
    tile = pl.BlockSpec((nb, t, d), lambda i, s: (i, s, 0))
    kvs = pl.BlockSpec((None, nb, N_MEM, d), lambda i, s: (kv_layer, i, 0, 0))
    names = ('g_attn_pre', 'w_q', 'w_o', 'g_attn_post')
    return pl.pallas_call(
        _attn_kernel,
        grid=grid,
        in_specs=[tile, kvs, kvs] + [_const_spec(w[n].shape, layer) for n in names],
        out_specs=tile,
        out_shape=jax.ShapeDtypeStruct(x.shape, _F32),
        scratch_shapes=[pltpu.VMEM((nb, N_MEM, d), _BF16), pltpu.VMEM((nb, N_MEM, d), _BF16)],
        compiler_params=_params(),
        name="attn",
    )(x, mem_k, mem_v, *[w[n] for n in names])


def _ffn_kernel(x_ref, fh_ref, gpre_ref, wup_ref, cw_ref, cb_ref, wdown_ref, gpost_ref,
                y_ref, fnew_ref, fcarry):
    nb, t, d = x_ref.shape
    r = nb * t

    @pl.when(pl.program_id(1) == 0)
    def _():
        fcarry[...] = fh_ref[...]

    x = x_ref[...].reshape(r, d)
    h = _rms(x, gpre_ref[...]).astype(_BF16)

    def conv_cols(lo):
        cols = slice(lo, lo + FFN_CHUNK)
        up = _dot(h, wup_ref[:, cols])
        ext = jnp.concatenate([fcarry[:, :, cols], up.reshape(nb, t, FFN_CHUNK)], axis=1)
        r1 = _shift_rows(ext, 1, HIST_PAD).reshape(r, FFN_CHUNK)
        r2 = _shift_rows(ext, 2, HIST_PAD).reshape(r, FFN_CHUNK)
        new_hist = ext[:, t:, :]
        fcarry[:, :, cols] = new_hist
        fnew_ref[:, :, cols] = new_hist
        return r2 * cw_ref[0:1, cols] + r1 * cw_ref[1:2, cols] + up * cw_ref[2:3, cols] + cb_ref[:, cols]

    acc = jnp.zeros((r, d), _F32)
    for c in range(D_FF // FFN_CHUNK):
        gate = conv_cols(c * FFN_CHUNK)
        value = conv_cols(D_FF + c * FFN_CHUNK)
        hid = (gate * jax.nn.sigmoid(gate) * value).astype(_BF16)
        acc = acc + _dot(hid, wdown_ref[c * FFN_CHUNK:(c + 1) * FFN_CHUNK, :])
    y_ref[...] = (x + _rms(acc, gpost_ref[...])).reshape(nb, t, d)


def _ffn_call(x, ffn_hist, hist_layer, layer, w, *, nb, t):
    b, seq, d = x.shape
    grid = (b // nb, seq // t)
    tile = pl.BlockSpec((nb, t, d), lambda i, s: (i, s, 0))
    fhs = pl.BlockSpec((None, nb, HIST_PAD, 2 * D_FF), lambda i, s: (hist_layer, i, 0, 0))
    fos = pl.BlockSpec((nb, HIST_PAD, 2 * D_FF), lambda i, s: (i, 0, 0))
    names = ('g_ffn_pre', 'w_up', 'ffn_conv_w', 'ffn_conv_b', 'w_down', 'g_ffn_post')
    return pl.pallas_call(
        _ffn_kernel,
        grid=grid,
        in_specs=[tile, fhs] + [_const_spec(w[n].shape, layer) for n in names],
        out_specs=[tile, fos],
        out_shape=[jax.ShapeDtypeStruct(x.shape, _F32),
                   jax.ShapeDtypeStruct((b, HIST_PAD, 2 * D_FF), _F32)],
        scratch_shapes=[pltpu.VMEM((nb, HIST_PAD, 2 * D_FF), _F32)],
        compiler_params=_params(),
        name="ffn",
    )(x, ffn_hist, *[w[n] for n in names])


def _memkv_kernel(mem_ref, g_ref, wk_ref, wv_ref, k_ref, v_ref):
    nb, m, d = mem_ref.shape
    x = mem_ref[...].reshape(nb * m, d)
    h = _rms(x, g_ref[...]).astype(_BF16)
    k_ref[...] = _dot(h, wk_ref[...]).reshape(nb, m, d)
    v_ref[...] = _dot(h, wv_ref[...]).reshape(nb, m, d)


def _memkv_call(mem, g_mem, w_k, w_v, *, nb):
    b, m, d = mem.shape
    depth = g_mem.shape[0]
    wspec = pl.BlockSpec((None, d, d), lambda l, i: (l, 0, 0))
    ospec = pl.BlockSpec((None, nb, m, d), lambda l, i: (l, i, 0, 0))
    return pl.pallas_call(
        _memkv_kernel,
        grid=(depth, b // nb),
        in_specs=[pl.BlockSpec((nb, m, d), lambda l, i: (i, 0, 0)),
                  pl.BlockSpec((None, 1, d), lambda l, i: (l, 0, 0)), wspec, wspec],
        out_specs=[ospec, ospec],
        out_shape=[jax.ShapeDtypeStruct((depth, b, m, d), _F32)] * 2,
        compiler_params=_params(),
        name="memkv",
    )(mem, g_mem, w_k, w_v)


def _pad_hist(h, pad):
    return jnp.pad(h, ((0, 0), (0, 0), (pad - h.shape[2], 0), (0, 0)))


def kernel(x_prompt, x_sample, mem_prompt, cache_mem_k, cache_mem_v, state_pool, state_conv, state_ffn_conv, g_mix_pre, g_mix_post, w_in, w_pool, pool_scale, conv_w, conv_b, w_out, g_attn_pre, g_attn_post, g_mem, w_q, w_k, w_v, w_o, g_ffn_pre, g_ffn_post, w_up, ffn_conv_w, ffn_conv_b, w_down):
    depth = w_in.shape[0]
    bp, seq, d = x_prompt.shape
    bs, dec_seq, _ = x_sample.shape

    def row(a):
        return a[:, None, :]

    w = {
        'g_mix_pre': row(g_mix_pre), 'g_mix_post': row(g_mix_post),
        'w_in': w_in.astype(_BF16), 'w_pool': w_pool.astype(_BF16), 'pool_scale': row(pool_scale),
        'conv_w': conv_w, 'conv_b': row(conv_b), 'w_out': w_out.astype(_BF16),
        'g_attn_pre': row(g_attn_pre), 'g_attn_post': row(g_attn_post),
        'w_q': (w_q.reshape(depth, d, d) * (HEAD_DIM ** -0.5)).astype(_BF16),
        'w_o': w_o.reshape(depth, d, d).astype(_BF16),
        'g_ffn_pre': row(g_ffn_pre), 'g_ffn_post': row(g_ffn_post),
        'w_up': w_up.astype(_BF16), 'ffn_conv_w': ffn_conv_w, 'ffn_conv_b': row(ffn_conv_b),
        'w_down': w_down.astype(_BF16),
    }

    mk_p, mv_p = _memkv_call(mem_prompt, row(g_mem), w_k.reshape(depth, d, d).astype(_BF16),
                             w_v.reshape(depth, d, d).astype(_BF16), nb=2)
    mk_s = cache_mem_k.reshape(depth, bs, N_MEM, d)
    mv_s = cache_mem_v.reshape(depth, bs, N_MEM, d)

    zero_pool = jnp.zeros((1, bp, POOL_PAD, POOL_WIDTH), _F32)
    zero_conv = jnp.zeros((1, bp, HIST_PAD, CONV_WIDTH), _F32)
    zero_ffn = jnp.zeros((1, bp, HIST_PAD, 2 * D_FF), _F32)
    pool_s_in = _pad_hist(state_pool, POOL_PAD)
    conv_s_in = _pad_hist(state_conv, HIST_PAD)
    ffn_s_in = _pad_hist(state_ffn_conv, HIST_PAD)

    tp = 512
    yp, ys = x_prompt, x_sample
    pool_p, conv_p, ffn_p, pool_s, conv_s, ffn_s = [], [], [], [], [], []
    for l in range(depth):
        yp, pn, cn = _mixer_call(yp, zero_pool, zero_conv, 0, l, w, nb=1, t=tp, start_pos=0)
        yp = _attn_call(yp, mk_p, mv_p, l, l, w, nb=1, t=tp)
        yp, fn = _ffn_call(yp, zero_ffn, 0, l, w, nb=1, t=tp)
        pool_p.append(pn[:, 1:])
        conv_p.append(cn[:, HIST_PAD - 2:])
        ffn_p.append(fn[:, HIST_PAD - 2:])
        ys, pn, cn = _mixer_call(ys, pool_s_in, conv_s_in, l, l, w, nb=8, t=dec_seq, start_pos=PAST_LEN)
        ys = _attn_call(ys, mk_s, mv_s, l, l, w, nb=4, t=dec_seq)
        ys, fn = _ffn_call(ys, ffn_s_in, l, l, w, nb=8, t=dec_seq)
        pool_s.append(pn[:, 1:])
        conv_s.append(cn[:, HIST_PAD - 2:])
        ffn_s.append(fn[:, HIST_PAD - 2:])

    shape5 = (depth, bp, N_MEM, N_HEADS, HEAD_DIM)
    return (yp, ys, mk_p.reshape(shape5), mv_p.reshape(shape5), jnp.stack(pool_p), jnp.stack(conv_p),
            jnp.stack(ffn_p), jnp.stack(pool_s), jnp.stack(conv_s), jnp.stack(ffn_s))
```

```python
import functools

import jax
import jax.numpy as jnp
from jax import lax
from jax.experimental import pallas as pl
from jax.experimental.pallas import tpu as pltpu

LANES = 128
D_MODEL = 1024
POOL_WIDTH = 512
CONV_WIDTH = 512
POOL_WINDOWS = (2, 4, 8, 16)
POOL_GROUP = 128
POOL_HIST = 15
POOL_PAD = 16
HIST_PAD = 8
CONV_K = 3
N_MEM = 256
N_HEADS = 4
HEAD_DIM = 256
D_FF = 2816
FFN_CHUNK = 256
EPS = 1e-6
PAST_LEN = 2048

VMEM_LIMIT_BYTES = 56 * 1024 * 1024

_BF16 = jnp.bfloat16
_F32 = jnp.float32


def _rms(x, g):
    ms = jnp.mean(x * x, axis=-1, keepdims=True)
    return x * lax.rsqrt(ms + EPS) * g


def _dot(a, b):
    return jnp.dot(a, b, preferred_element_type=_F32)


def _tree_sum(xs):
    while len(xs) > 1:
        xs = [xs[i] + xs[i + 1] for i in range(0, len(xs) - 1, 2)] + ([xs[-1]] if len(xs) % 2 else [])
    return xs[0]


def _conv3_tile(buf, j, t, w_ref, b_ref, cols):
    taps = [buf[j, :, pl.ds(HIST_PAD - (CONV_K - 1) + k, t), :] * w_ref[k:k + 1, cols]
            for k in range(CONV_K)]
    return taps[0] + taps[1] + (taps[2] + b_ref[:, cols])


def _const_spec(shape, layer):
    nd = len(shape)
    return pl.BlockSpec((None,) + tuple(shape[1:]), lambda b, s: (layer,) + (0,) * (nd - 1),
                        pipeline_mode=pl.Buffered(1))


def _params():
    return pltpu.CompilerParams(dimension_semantics=("arbitrary", "arbitrary"),
                                vmem_limit_bytes=VMEM_LIMIT_BYTES)


def _mixer_kernel(x_ref, ph_ref, ch_ref, gpre_ref, win_ref, wpool_ref, pscale_ref, cw_ref, cb_ref,
                  wout_ref, gpost_ref, y_ref, pnew_ref, cnew_ref, pbuf, cbuf, *, start_pos):
    nb, t, d = x_ref.shape
    r = nb * t
    s = pl.program_id(1)
    n_tiles = POOL_WIDTH // LANES

    @pl.when(s == 0)
    def _():
        for j in range(n_tiles):
            cols = slice(j * LANES, (j + 1) * LANES)
            pbuf[j, :, 0:POOL_PAD, :] = ph_ref[:, :, cols]
            cbuf[j, :, 0:HIST_PAD, :] = ch_ref[:, :, cols]

    x = x_ref[...].reshape(r, d)
    h = _rms(x, gpre_ref[...]).astype(_BF16)
    proj = _dot(h, win_ref[...])
    u = proj[:, :POOL_WIDTH]
    b_gate = proj[:, POOL_WIDTH:POOL_WIDTH + CONV_WIDTH]
    cv = proj[:, POOL_WIDTH + CONV_WIDTH:POOL_WIDTH + 2 * CONV_WIDTH] * proj[:, POOL_WIDTH + 2 * CONV_WIDTH:]
    for j in range(n_tiles):
        cols = slice(j * LANES, (j + 1) * LANES)
        pbuf[j, :, POOL_PAD:, :] = u[:, cols].reshape(nb, t, LANES)
        cbuf[j, :, HIST_PAD:, :] = cv[:, cols].reshape(nb, t, LANES)

    pos = start_pos + s * t + lax.broadcasted_iota(jnp.int32, (nb, t, 1), 1)
    ys = []
    for g, w in enumerate(POOL_WINDOWS):
        cols = slice(g * POOL_GROUP, (g + 1) * POOL_GROUP)
        if w <= 8:
            win = _tree_sum([pbuf[g, :, pl.ds(POOL_PAD - i, t), :] for i in range(w)])
        else:
            s8 = _tree_sum([pbuf[g, :, pl.ds(8 - i, t + 8), :] for i in range(8)])
            win = s8[:, 8:, :] + s8[:, :t, :]
        inv_cnt = 1.0 / jnp.minimum(w, pos + 1).astype(_F32)
        dlt = ((win * inv_cnt).reshape(r, POOL_GROUP) - u[:, cols]).astype(_BF16)
        ys.append(_dot(dlt, wpool_ref[g]) * pscale_ref[:, cols])

    for j in range(n_tiles):
        cols = slice(j * LANES, (j + 1) * LANES)
        zc = _conv3_tile(cbuf, j, t, cw_ref, cb_ref, cols).reshape(r, LANES)
        ys.append(b_gate[:, cols] * zc)

    for j in range(n_tiles):
        cols = slice(j * LANES, (j + 1) * LANES)
        new_pool = pbuf[j, :, t:, :]
        new_conv = cbuf[j, :, t:, :]
        pnew_ref[:, :, cols] = new_pool
        cnew_ref[:, :, cols] = new_conv
        pbuf[j, :, 0:POOL_PAD, :] = new_pool
        cbuf[j, :, 0:HIST_PAD, :] = new_conv

    ycat = jnp.concatenate(ys, axis=-1).astype(_BF16)
    y = _dot(ycat, wout_ref[...])
    y_ref[...] = (x + _rms(y, gpost_ref[...])).reshape(nb, t, d)


def _mixer_call(x, pool_hist, conv_hist, hist_layer, layer, w, *, nb, t, start_pos):
    b, seq, d = x.shape
    grid = (b // nb, seq // t)
    tile = pl.BlockSpec((nb, t, d), lambda i, s: (i, s, 0))
    phs = pl.BlockSpec((None, nb, POOL_PAD, POOL_WIDTH), lambda i, s: (hist_layer, i, 0, 0))
    chs = pl.BlockSpec((None, nb, HIST_PAD, CONV_WIDTH), lambda i, s: (hist_layer, i, 0, 0))
    pos_ = pl.BlockSpec((nb, POOL_PAD, POOL_WIDTH), lambda i, s: (i, 0, 0))
    cos_ = pl.BlockSpec((nb, HIST_PAD, CONV_WIDTH), lambda i, s: (i, 0, 0))
    names = ('g_mix_pre', 'w_in', 'w_pool', 'pool_scale', 'conv_w', 'conv_b', 'w_out', 'g_mix_post')
    return pl.pallas_call(
        functools.partial(_mixer_kernel, start_pos=start_pos),
        grid=grid,
        in_specs=[tile, phs, chs] + [_const_spec(w[n].shape, layer) for n in names],
        out_specs=[tile, pos_, cos_],
        out_shape=[jax.ShapeDtypeStruct(x.shape, _F32),
                   jax.ShapeDtypeStruct((b, POOL_PAD, POOL_WIDTH), _F32),
                   jax.ShapeDtypeStruct((b, HIST_PAD, CONV_WIDTH), _F32)],
        scratch_shapes=[pltpu.VMEM((POOL_WIDTH // LANES, nb, POOL_PAD + t, LANES), _F32),
                        pltpu.VMEM((CONV_WIDTH // LANES, nb, HIST_PAD + t, LANES), _F32)],
        compiler_params=_params(),
        name="mixer",
    )(x, pool_hist, conv_hist, *[w[n] for n in names])


def _attn_kernel(x_ref, k_ref, v_ref, gpre_ref, wq_ref, wo_ref, gpost_ref, y_ref, k_sc, v_sc):
    nb, t, d = x_ref.shape
    r = nb * t

    @pl.when(pl.program_id(1) == 0)
    def _():
        k_sc[...] = k_ref[...].astype(_BF16)
        v_sc[...] = v_ref[...].astype(_BF16)

    x = x_ref[...].reshape(r, d)
    h = _rms(x, gpre_ref[...]).astype(_BF16)
    q = _dot(h, wq_ref[...]).astype(_BF16)
    rows = []
    for n in range(nb):
        heads = []
        for hd in range(N_HEADS):
            cols = slice(hd * HEAD_DIM, (hd + 1) * HEAD_DIM)
            qh = q[n * t:(n + 1) * t, cols]
            sc = lax.dot_general(qh, k_sc[n, :, cols], (((1,), (1,)), ((), ())),
                                 preferred_element_type=_F32)
            e = jnp.exp(sc - jnp.max(sc, axis=-1, keepdims=True))
            p = (e / jnp.sum(e, axis=-1, keepdims=True)).astype(_BF16)
            heads.append(_dot(p, v_sc[n, :, cols]))
        rows.append(jnp.concatenate(heads, axis=-1))
    o = (rows[0] if nb == 1 else jnp.concatenate(rows, axis=0)).astype(_BF16)
    y = _dot(o, wo_ref[...])
    y_ref[...] = (x + _rms(y, gpost_ref[...])).reshape(nb, t, d)


def _attn_call(x, mem_k, mem_v, kv_layer, layer, w, *, nb, t):
    b, seq, d = x.shape
    grid = (b // nb, seq // t)
    tile = pl.BlockSpec((nb, t, d), lambda i, s: (i, s, 0))
    kvs = pl.BlockSpec((None, nb, N_MEM, d), lambda i, s: (kv_layer, i, 0, 0))
    names = ('g_attn_pre', 'w_q', 'w_o', 'g_attn_post')
    return pl.pallas_call(
        _attn_kernel,
        grid=grid,
        in_specs=[tile, kvs, kvs] + [_const_spec(w[n].shape, layer) for n in names],
        out_specs=tile,
        out_shape=jax.ShapeDtypeStruct(x.shape, _F32),
        scratch_shapes=[pltpu.VMEM((nb, N_MEM, d), _BF16), pltpu.VMEM((nb, N_MEM, d), _BF16)],
        compiler_params=_params(),
        name="attn",
    )(x, mem_k, mem_v, *[w[n] for n in names])


def _ffn_kernel(x_ref, fh_ref, gpre_ref, wup_ref, cw_ref, cb_ref, wdown_ref, gpost_ref,
                y_ref, fnew_ref, ubuf, hid):
    nb, t, d = x_ref.shape
    r = nb * t
    n_tiles = 2 * D_FF // LANES
    per_chunk = FFN_CHUNK // LANES

    @pl.when(pl.program_id(1) == 0)
    def _():
        for j in range(n_tiles):
            ubuf[j, :, 0:HIST_PAD, :] = fh_ref[:, :, j * LANES:(j + 1) * LANES]

    x = x_ref[...].reshape(r, d)
    h = _rms(x, gpre_ref[...]).astype(_BF16)

    def conv_cols(lo):
        up = _dot(h, wup_ref[:, lo:lo + FFN_CHUNK])
        outs = []
        for k in range(per_chunk):
            j = lo // LANES + k
            ubuf[j, :, HIST_PAD:, :] = up[:, k * LANES:(k + 1) * LANES].reshape(nb, t, LANES)
            cols = slice(j * LANES, (j + 1) * LANES)
            outs.append(_conv3_tile(ubuf, j, t, cw_ref, cb_ref, cols).reshape(r, LANES))
        return jnp.concatenate(outs, axis=-1)

    for c in range(D_FF // FFN_CHUNK):
        gate = conv_cols(c * FFN_CHUNK)
        value = conv_cols(D_FF + c * FFN_CHUNK)
        hid[:, c * FFN_CHUNK:(c + 1) * FFN_CHUNK] = (gate * jax.nn.sigmoid(gate) * value).astype(_BF16)

    for j in range(n_tiles):
        new_hist = ubuf[j, :, t:, :]
        fnew_ref[:, :, j * LANES:(j + 1) * LANES] = new_hist
        ubuf[j, :, 0:HIST_PAD, :] = new_hist

    y = _dot(hid[...], wdown_ref[...])
    y_ref[...] = (x + _rms(y, gpost_ref[...])).reshape(nb, t, d)


def _ffn_call(x, ffn_hist, hist_layer, layer, w, *, nb, t):
    b, seq, d = x.shape
    grid = (b // nb, seq // t)
    tile = pl.BlockSpec((nb, t, d), lambda i, s: (i, s, 0))
    fhs = pl.BlockSpec((None, nb, HIST_PAD, 2 * D_FF), lambda i, s: (hist_layer, i, 0, 0))
    fos = pl.BlockSpec((nb, HIST_PAD, 2 * D_FF), lambda i, s: (i, 0, 0))
    names = ('g_ffn_pre', 'w_up', 'ffn_conv_w', 'ffn_conv_b', 'w_down', 'g_ffn_post')
    return pl.pallas_call(
        _ffn_kernel,
        grid=grid,
        in_specs=[tile, fhs] + [_const_spec(w[n].shape, layer) for n in names],
        out_specs=[tile, fos],
        out_shape=[jax.ShapeDtypeStruct(x.shape, _F32),
                   jax.ShapeDtypeStruct((b, HIST_PAD, 2 * D_FF), _F32)],
        scratch_shapes=[pltpu.VMEM((2 * D_FF // LANES, nb, HIST_PAD + t, LANES), _F32),
                        pltpu.VMEM((nb * t, D_FF), _BF16)],
        compiler_params=_params(),
        name="ffn",
    )(x, ffn_hist, *[w[n] for n in names])


def _memkv_kernel(mem_ref, g_ref, wk_ref, wv_ref, k_ref, v_ref):
    nb, m, d = mem_ref.shape
    x = mem_ref[...].reshape(nb * m, d)
    h = _rms(x, g_ref[...]).astype(_BF16)
    k_ref[...] = _dot(h, wk_ref[...]).reshape(nb, m, d)
    v_ref[...] = _dot(h, wv_ref[...]).reshape(nb, m, d)


def _memkv_call(mem, g_mem, w_k, w_v, *, nb):
    b, m, d = mem.shape
    depth = g_mem.shape[0]
    wspec = pl.BlockSpec((None, d, d), lambda l, i: (l, 0, 0))
    ospec = pl.BlockSpec((None, nb, m, d), lambda l, i: (l, i, 0, 0))
    return pl.pallas_call(
        _memkv_kernel,
        grid=(depth, b // nb),
        in_specs=[pl.BlockSpec((nb, m, d), lambda l, i: (i, 0, 0)),
                  pl.BlockSpec((None, 1, d), lambda l, i: (l, 0, 0)), wspec, wspec],
        out_specs=[ospec, ospec],
        out_shape=[jax.ShapeDtypeStruct((depth, b, m, d), _F32)] * 2,
        compiler_params=_params(),
        name="memkv",
    )(mem, g_mem, w_k, w_v)


def _pad_hist(h, pad):
    return jnp.pad(h, ((0, 0), (0, 0), (pad - h.shape[2], 0), (0, 0)))


def kernel(x_prompt, x_sample, mem_prompt, cache_mem_k, cache_mem_v, state_pool, state_conv, state_ffn_conv, g_mix_pre, g_mix_post, w_in, w_pool, pool_scale, conv_w, conv_b, w_out, g_attn_pre, g_attn_post, g_mem, w_q, w_k, w_v, w_o, g_ffn_pre, g_ffn_post, w_up, ffn_conv_w, ffn_conv_b, w_down):
    depth = w_in.shape[0]
    bp, seq, d = x_prompt.shape
    bs, dec_seq, _ = x_sample.shape

    def row(a):
        return a[:, None, :]

    w = {
        'g_mix_pre': row(g_mix_pre), 'g_mix_post': row(g_mix_post),
        'w_in': w_in.astype(_BF16), 'w_pool': w_pool.astype(_BF16), 'pool_scale': row(pool_scale),
        'conv_w': conv_w, 'conv_b': row(conv_b), 'w_out': w_out.astype(_BF16),
        'g_attn_pre': row(g_attn_pre), 'g_attn_post': row(g_attn_post),
        'w_q': (w_q.reshape(depth, d, d) * (HEAD_DIM ** -0.5)).astype(_BF16),
        'w_o': w_o.reshape(depth, d, d).astype(_BF16),
        'g_ffn_pre': row(g_ffn_pre), 'g_ffn_post': row(g_ffn_post),
        'w_up': w_up.astype(_BF16), 'ffn_conv_w': ffn_conv_w, 'ffn_conv_b': row(ffn_conv_b),
        'w_down': w_down.astype(_BF16),
    }

    mk_p, mv_p = _memkv_call(mem_prompt, row(g_mem), w_k.reshape(depth, d, d).astype(_BF16),
                             w_v.reshape(depth, d, d).astype(_BF16), nb=2)
    mk_s = cache_mem_k.reshape(depth, bs, N_MEM, d)
    mv_s = cache_mem_v.reshape(depth, bs, N_MEM, d)

    zero_pool = jnp.zeros((1, bp, POOL_PAD, POOL_WIDTH), _F32)
    zero_conv = jnp.zeros((1, bp, HIST_PAD, CONV_WIDTH), _F32)
    zero_ffn = jnp.zeros((1, bp, HIST_PAD, 2 * D_FF), _F32)
    pool_s_in = _pad_hist(state_pool, POOL_PAD)
    conv_s_in = _pad_hist(state_conv, HIST_PAD)
    ffn_s_in = _pad_hist(state_ffn_conv, HIST_PAD)

    tp = 512
    yp, ys = x_prompt, x_sample
    pool_p, conv_p, ffn_p, pool_s, conv_s, ffn_s = [], [], [], [], [], []
    for l in range(depth):
        yp, pn, cn = _mixer_call(yp, zero_pool, zero_conv, 0, l, w, nb=1, t=tp, start_pos=0)
        yp = _attn_call(yp, mk_p, mv_p, l, l, w, nb=1, t=tp)
        yp, fn = _ffn_call(yp, zero_ffn, 0, l, w, nb=1, t=tp)
        pool_p.append(pn[:, 1:])
        conv_p.append(cn[:, HIST_PAD - 2:])
        ffn_p.append(fn[:, HIST_PAD - 2:])
        ys, pn, cn = _mixer_call(ys, pool_s_in, conv_s_in, l, l, w, nb=8, t=dec_seq, start_pos=PAST_LEN)
        ys = _attn_call(ys, mk_s, mv_s, l, l, w, nb=4, t=dec_seq)
        ys, fn = _ffn_call(ys, ffn_s_in, l, l, w, nb=8, t=dec_seq)
        pool_s.append(pn[:, 1:])
        conv_s.append(cn[:, HIST_PAD - 2:])
        ffn_s.append(fn[:, HIST_PAD - 2:])

    shape5 = (depth, bp, N_MEM, N_HEADS, HEAD_DIM)
    return (yp, ys, mk_p.reshape(shape5), mv_p.reshape(shape5), jnp.stack(pool_p), jnp.stack(conv_p),
            jnp.stack(ffn_p), jnp.stack(pool_s), jnp.stack(conv_s), jnp.stack(ffn_s))
```

```python
import functools

import jax
import jax.numpy as jnp
from jax import lax
from jax.experimental import pallas as pl
from jax.experimental.pallas import tpu as pltpu

LANES = 128
D_MODEL = 1024
POOL_WIDTH = 512
CONV_WIDTH = 512
POOL_WINDOWS = (2, 4, 8, 16)
POOL_GROUP = 128
POOL_HIST = 15
POOL_PAD = 16
HIST_PAD = 8
CONV_K = 3
N_MEM = 256
N_HEADS = 4
HEAD_DIM = 256
D_FF = 2816
FFN_CHUNK = 256
EPS = 1e-6
PAST_LEN = 2048

VMEM_LIMIT_BYTES = 56 * 1024 * 1024

_BF16 = jnp.bfloat16
_F32 = jnp.float32


def _rms(x, g):
    ms = jnp.mean(x * x, axis=-1, keepdims=True)
    return x * lax.rsqrt(ms + EPS) * g


def _dot(a, b):
    return jnp.dot(a, b, preferred_element_type=_F32)


def _tree_sum(xs):
    while len(xs) > 1:
        xs = [xs[i] + xs[i + 1] for i in range(0, len(xs) - 1, 2)] + ([xs[-1]] if len(xs) % 2 else [])
    return xs[0]


def _conv3_tile(buf, j, t, w_ref, b_ref, cols):
    taps = [buf[j, :, pl.ds(HIST_PAD - (CONV_K - 1) + k, t), :] * w_ref[k:k + 1, cols]
            for k in range(CONV_K)]
    return taps[0] + taps[1] + (taps[2] + b_ref[:, cols])


def _const_spec(shape, layer):
    nd = len(shape)
    return pl.BlockSpec((None,) + tuple(shape[1:]), lambda b, s: (layer,) + (0,) * (nd - 1),
                        pipeline_mode=pl.Buffered(1))


def _params():
    return pltpu.CompilerParams(dimension_semantics=("arbitrary", "arbitrary"),
                                vmem_limit_bytes=VMEM_LIMIT_BYTES)


def _mixer_kernel(x_ref, ph_ref, ch_ref, gpre_ref, win_ref, wpool_ref, pscale_ref, cw_ref, cb_ref,
                  wout_ref, gpost_ref, y_ref, pnew_ref, cnew_ref, pbuf, cbuf, *, start_pos):
    nb, t, d = x_ref.shape
    r = nb * t
    s = pl.program_id(1)
    n_tiles = POOL_WIDTH // LANES

    @pl.when(s == 0)
    def _():
        for j in range(n_tiles):
            cols = slice(j * LANES, (j + 1) * LANES)
            pbuf[j, :, 0:POOL_PAD, :] = ph_ref[:, :, cols]
            cbuf[j, :, 0:HIST_PAD, :] = ch_ref[:, :, cols]

    x = x_ref[...].reshape(r, d)
    h = _rms(x, gpre_ref[...]).astype(_BF16)
    proj = _dot(h, win_ref[...])
    u = proj[:, :POOL_WIDTH]
    b_gate = proj[:, POOL_WIDTH:POOL_WIDTH + CONV_WIDTH]
    cv = proj[:, POOL_WIDTH + CONV_WIDTH:POOL_WIDTH + 2 * CONV_WIDTH] * proj[:, POOL_WIDTH + 2 * CONV_WIDTH:]
    for j in range(n_tiles):
        cols = slice(j * LANES, (j + 1) * LANES)
        pbuf[j, :, POOL_PAD:, :] = u[:, cols].reshape(nb, t, LANES)
        cbuf[j, :, HIST_PAD:, :] = cv[:, cols].reshape(nb, t, LANES)

    pos = start_pos + s * t + lax.broadcasted_iota(jnp.int32, (nb, t, 1), 1)
    ys = []
    for g, w in enumerate(POOL_WINDOWS):
        cols = slice(g * POOL_GROUP, (g + 1) * POOL_GROUP)
        if w <= 8:
            win = _tree_sum([pbuf[g, :, pl.ds(POOL_PAD - i, t), :] for i in range(w)])
        else:
            s8 = _tree_sum([pbuf[g, :, pl.ds(8 - i, t + 8), :] for i in range(8)])
            win = s8[:, 8:, :] + s8[:, :t, :]
        inv_cnt = 1.0 / jnp.minimum(w, pos + 1).astype(_F32)
        dlt = ((win * inv_cnt).reshape(r, POOL_GROUP) - u[:, cols]).astype(_BF16)
        ys.append(_dot(dlt, wpool_ref[g]) * pscale_ref[:, cols])

    for j in range(n_tiles):
        cols = slice(j * LANES, (j + 1) * LANES)
        zc = _conv3_tile(cbuf, j, t, cw_ref, cb_ref, cols).reshape(r, LANES)
        ys.append(b_gate[:, cols] * zc)

    for j in range(n_tiles):
        cols = slice(j * LANES, (j + 1) * LANES)
        new_pool = pbuf[j, :, t:, :]
        new_conv = cbuf[j, :, t:, :]
        pnew_ref[:, :, cols] = new_pool
        cnew_ref[:, :, cols] = new_conv
        pbuf[j, :, 0:POOL_PAD, :] = new_pool
        cbuf[j, :, 0:HIST_PAD, :] = new_conv

    ycat = jnp.concatenate(ys, axis=-1).astype(_BF16)
    y = _dot(ycat, wout_ref[...])
    y_ref[...] = (x + _rms(y, gpost_ref[...])).reshape(nb, t, d)


def _mixer_call(x, pool_hist, conv_hist, hist_layer, layer, w, *, nb, t, start_pos):
    b, seq, d = x.shape
    grid = (b // nb, seq // t)
    tile = pl.BlockSpec((nb, t, d), lambda i, s: (i, s, 0))
    phs = pl.BlockSpec((None, nb, POOL_PAD, POOL_WIDTH), lambda i, s: (hist_layer, i, 0, 0))
    chs = pl.BlockSpec((None, nb, HIST_PAD, CONV_WIDTH), lambda i, s: (hist_layer, i, 0, 0))
    pos_ = pl.BlockSpec((nb, POOL_PAD, POOL_WIDTH), lambda i, s: (i, 0, 0))
    cos_ = pl.BlockSpec((nb, HIST_PAD, CONV_WIDTH), lambda i, s: (i, 0, 0))
    names = ('g_mix_pre', 'w_in', 'w_pool', 'pool_scale', 'conv_w', 'conv_b', 'w_out', 'g_mix_post')
    return pl.pallas_call(
        functools.partial(_mixer_kernel, start_pos=start_pos),
        grid=grid,
        in_specs=[tile, phs, chs] + [_const_spec(w[n].shape, layer) for n in names],
        out_specs=[tile, pos_, cos_],
        out_shape=[jax.ShapeDtypeStruct(x.shape, _F32),
                   jax.ShapeDtypeStruct((b, POOL_PAD, POOL_WIDTH), _F32),
                   jax.ShapeDtypeStruct((b, HIST_PAD, CONV_WIDTH), _F32)],
        scratch_shapes=[pltpu.VMEM((POOL_WIDTH // LANES, nb, POOL_PAD + t, LANES), _F32),
                        pltpu.VMEM((CONV_WIDTH // LANES, nb, HIST_PAD + t, LANES), _F32)],
        compiler_params=_params(),
        name="mixer",
    )(x, pool_hist, conv_hist, *[w[n] for n in names])


def _attn_kernel(x_ref, k_ref, v_ref, gpre_ref, wq_ref, wo_ref, gpost_ref, y_ref, *kv_scratch):
    nb, t, d = x_ref.shape
    r = nb * t

    if kv_scratch:
        k_sc, v_sc = kv_scratch

        @pl.when(pl.program_id(1) == 0)
        def _():
            for n in range(nb):
                for hd in range(N_HEADS):
                    cols = slice(hd * HEAD_DIM, (hd + 1) * HEAD_DIM)
                    k_sc[n, :, cols] = k_ref[n, :, hd, :].astype(_BF16)
                    v_sc[n, :, cols] = v_ref[n, :, hd, :].astype(_BF16)
    else:
        k_sc, v_sc = k_ref, v_ref

    x = x_ref[...].reshape(r, d)
    h = _rms(x, gpre_ref[...]).astype(_BF16)
    q = _dot(h, wq_ref[...]).astype(_BF16)
    rows = []
    for n in range(nb):
        heads = []
        for hd in range(N_HEADS):
            cols = slice(hd * HEAD_DIM, (hd + 1) * HEAD_DIM)
            qh = q[n * t:(n + 1) * t, cols]
            sc = lax.dot_general(qh, k_sc[n, :, cols], (((1,), (1,)), ((), ())),
                                 preferred_element_type=_F32)
            e = jnp.exp(sc - jnp.max(sc, axis=-1, keepdims=True))
            p = (e / jnp.sum(e, axis=-1, keepdims=True)).astype(_BF16)
            heads.append(_dot(p, v_sc[n, :, cols]))
        rows.append(jnp.concatenate(heads, axis=-1))
    o = (rows[0] if nb == 1 else jnp.concatenate(rows, axis=0)).astype(_BF16)
    y = _dot(o, wo_ref[...])
    y_ref[...] = (x + _rms(y, gpost_ref[...])).reshape(nb, t, d)


def _attn_call(x, mem_k, mem_v, kv_layer, layer, w, *, nb, t):
    b, seq, d = x.shape
    grid = (b // nb, seq // t)
    tile = pl.BlockSpec((nb, t, d), lambda i, s: (i, s, 0))
    if mem_k.ndim == 4:
        kvs = pl.BlockSpec((None, nb, N_MEM, d), lambda i, s: (kv_layer, i, 0, 0))
        scratch = []
    else:
        kvs = pl.BlockSpec((None, nb, N_MEM, N_HEADS, HEAD_DIM), lambda i, s: (kv_layer, i, 0, 0, 0))
        scratch = [pltpu.VMEM((nb, N_MEM, d), _BF16), pltpu.VMEM((nb, N_MEM, d), _BF16)]
    names = ('g_attn_pre', 'w_q', 'w_o', 'g_attn_post')
    return pl.pallas_call(
        _attn_kernel,
        grid=grid,
        in_specs=[tile, kvs, kvs] + [_const_spec(w[n].shape, layer) for n in names],
        out_specs=tile,
        out_shape=jax.ShapeDtypeStruct(x.shape, _F32),
        scratch_shapes=scratch,
        compiler_params=_params(),
        name="attn",
    )(x, mem_k, mem_v, *[w[n] for n in names])


def _ffn_kernel(x_ref, fh_ref, gpre_ref, wup_ref, cw_ref, cb_ref, wdown_ref, gpost_ref,
                y_ref, fnew_ref, ubuf, hid):
    nb, t, d = x_ref.shape
    r = nb * t
    n_tiles = 2 * D_FF // LANES
    per_chunk = FFN_CHUNK // LANES

    @pl.when(pl.program_id(1) == 0)
    def _():
        for j in range(n_tiles):
            ubuf[j, :, 0:HIST_PAD, :] = fh_ref[:, :, j * LANES:(j + 1) * LANES]

    x = x_ref[...].reshape(r, d)
    h = _rms(x, gpre_ref[...]).astype(_BF16)

    def conv_cols(lo):
        up = _dot(h, wup_ref[:, lo:lo + FFN_CHUNK])
        outs = []
        for k in range(per_chunk):
            j = lo // LANES + k
            ubuf[j, :, HIST_PAD:, :] = up[:, k * LANES:(k + 1) * LANES].reshape(nb, t, LANES)
            cols = slice(j * LANES, (j + 1) * LANES)
            outs.append(_conv3_tile(ubuf, j, t, cw_ref, cb_ref, cols).reshape(r, LANES))
        return jnp.concatenate(outs, axis=-1)

    for c in range(D_FF // FFN_CHUNK):
        gate = conv_cols(c * FFN_CHUNK)
        value = conv_cols(D_FF + c * FFN_CHUNK)
        hid[:, c * FFN_CHUNK:(c + 1) * FFN_CHUNK] = (gate * jax.nn.sigmoid(gate) * value).astype(_BF16)

    for j in range(n_tiles):
        new_hist = ubuf[j, :, t:, :]
        fnew_ref[:, :, j * LANES:(j + 1) * LANES] = new_hist
        ubuf[j, :, 0:HIST_PAD, :] = new_hist

    y = _dot(hid[...], wdown_ref[...])
    y_ref[...] = (x + _rms(y, gpost_ref[...])).reshape(nb, t, d)


def _ffn_call(x, ffn_hist, hist_layer, layer, w, *, nb, t):
    b, seq, d = x.shape
    grid = (b // nb, seq // t)
    tile = pl.BlockSpec((nb, t, d), lambda i, s: (i, s, 0))
    fhs = pl.BlockSpec((None, nb, HIST_PAD, 2 * D_FF), lambda i, s: (hist_layer, i, 0, 0))
    fos = pl.BlockSpec((nb, HIST_PAD, 2 * D_FF), lambda i, s: (i, 0, 0))
    names = ('g_ffn_pre', 'w_up', 'ffn_conv_w', 'ffn_conv_b', 'w_down', 'g_ffn_post')
    return pl.pallas_call(
        _ffn_kernel,
        grid=grid,
        in_specs=[tile, fhs] + [_const_spec(w[n].shape, layer) for n in names],
        out_specs=[tile, fos],
        out_shape=[jax.ShapeDtypeStruct(x.shape, _F32),
                   jax.ShapeDtypeStruct((b, HIST_PAD, 2 * D_FF), _F32)],
        scratch_shapes=[pltpu.VMEM((2 * D_FF // LANES, nb, HIST_PAD + t, LANES), _F32),
                        pltpu.VMEM((nb * t, D_FF), _BF16)],
        compiler_params=_params(),
        name="ffn",
    )(x, ffn_hist, *[w[n] for n in names])


def _memkv_kernel(mem_ref, g_ref, wk_ref, wv_ref, k5_ref, v5_ref, kb_ref, vb_ref):
    nb, m, d = mem_ref.shape
    x = mem_ref[...].reshape(nb * m, d)
    h = _rms(x, g_ref[...]).astype(_BF16)
    for w_ref, o5_ref, ob_ref in ((wk_ref, k5_ref, kb_ref), (wv_ref, v5_ref, vb_ref)):
        kv = _dot(h, w_ref[...])
        ob_ref[...] = kv.astype(_BF16).reshape(nb, m, d)
        for n in range(nb):
            for hd in range(N_HEADS):
                o5_ref[n, :, hd, :] = kv[n * m:(n + 1) * m, hd * HEAD_DIM:(hd + 1) * HEAD_DIM]


def _memkv_call(mem, g_mem, w_k, w_v, *, nb):
    b, m, d = mem.shape
    depth = g_mem.shape[0]
    wspec = pl.BlockSpec((None, d, d), lambda l, i: (l, 0, 0))
    o5spec = pl.BlockSpec((None, nb, m, N_HEADS, HEAD_DIM), lambda l, i: (l, i, 0, 0, 0))
    obspec = pl.BlockSpec((None, nb, m, d), lambda l, i: (l, i, 0, 0))
    return pl.pallas_call(
        _memkv_kernel,
        grid=(depth, b // nb),
        in_specs=[pl.BlockSpec((nb, m, d), lambda l, i: (i, 0, 0)),
                  pl.BlockSpec((None, 1, d), lambda l, i: (l, 0, 0)), wspec, wspec],
        out_specs=[o5spec, o5spec, obspec, obspec],
        out_shape=[jax.ShapeDtypeStruct((depth, b, m, N_HEADS, HEAD_DIM), _F32)] * 2
        + [jax.ShapeDtypeStruct((depth, b, m, d), _BF16)] * 2,
        compiler_params=_params(),
        name="memkv",
    )(mem, g_mem, w_k, w_v)


def _pad_hist(h, pad):
    return jnp.pad(h, ((0, 0), (0, 0), (pad - h.shape[2], 0), (0, 0)))


def kernel(x_prompt, x_sample, mem_prompt, cache_mem_k, cache_mem_v, state_pool, state_conv, state_ffn_conv, g_mix_pre, g_mix_post, w_in, w_pool, pool_scale, conv_w, conv_b, w_out, g_attn_pre, g_attn_post, g_mem, w_q, w_k, w_v, w_o, g_ffn_pre, g_ffn_post, w_up, ffn_conv_w, ffn_conv_b, w_down):
    depth = w_in.shape[0]
    bp, seq, d = x_prompt.shape
    bs, dec_seq, _ = x_sample.shape

    def row(a):
        return a[:, None, :]

    w = {
        'g_mix_pre': row(g_mix_pre), 'g_mix_post': row(g_mix_post),
        'w_in': w_in.astype(_BF16), 'w_pool': w_pool.astype(_BF16), 'pool_scale': row(pool_scale),
        'conv_w': conv_w, 'conv_b': row(conv_b), 'w_out': w_out.astype(_BF16),
        'g_attn_pre': row(g_attn_pre), 'g_attn_post': row(g_attn_post),
        'w_q': (w_q.reshape(depth, d, d) * (HEAD_DIM ** -0.5)).astype(_BF16),
        'w_o': w_o.reshape(depth, d, d).astype(_BF16),
        'g_ffn_pre': row(g_ffn_pre), 'g_ffn_post': row(g_ffn_post),
        'w_up': w_up.astype(_BF16), 'ffn_conv_w': ffn_conv_w, 'ffn_conv_b': row(ffn_conv_b),
        'w_down': w_down.astype(_BF16),
    }

    mk_out, mv_out, mk_p, mv_p = _memkv_call(
        mem_prompt, row(g_mem), w_k.reshape(depth, d, d).astype(_BF16),
        w_v.reshape(depth, d, d).astype(_BF16), nb=2)
    mk_s, mv_s = cache_mem_k, cache_mem_v

    zero_pool = jnp.zeros((1, bp, POOL_PAD, POOL_WIDTH), _F32)
    zero_conv = jnp.zeros((1, bp, HIST_PAD, CONV_WIDTH), _F32)
    zero_ffn = jnp.zeros((1, bp, HIST_PAD, 2 * D_FF), _F32)
    pool_s_in = _pad_hist(state_pool, POOL_PAD)
    conv_s_in = _pad_hist(state_conv, HIST_PAD)
    ffn_s_in = _pad_hist(state_ffn_conv, HIST_PAD)

    tp = 512
    yp, ys = x_prompt, x_sample
    pool_p, conv_p, ffn_p, pool_s, conv_s, ffn_s = [], [], [], [], [], []
    for l in range(depth):
        yp, pn, cn = _mixer_call(yp, zero_pool, zero_conv, 0, l, w, nb=1, t=tp, start_pos=0)
        yp = _attn_call(yp, mk_p, mv_p, l, l, w, nb=1, t=tp)
        yp, fn = _ffn_call(yp, zero_ffn, 0, l, w, nb=1, t=tp)
        pool_p.append(pn[:, 1:])
        conv_p.append(cn[:, HIST_PAD - 2:])
        ffn_p.append(fn[:, HIST_PAD - 2:])
        ys, pn, cn = _mixer_call(ys, pool_s_in, conv_s_in, l, l, w, nb=8, t=dec_seq, start_pos=PAST_LEN)
        ys = _attn_call(ys, mk_s, mv_s, l, l, w, nb=4, t=dec_seq)
        ys, fn = _ffn_call(ys, ffn_s_in, l, l, w, nb=8, t=dec_seq)
        pool_s.append(pn[:, 1:])
        conv_s.append(cn[:, HIST_PAD - 2:])
        ffn_s.append(fn[:, HIST_PAD - 2:])

    return (yp, ys, mk_out, mv_out, jnp.stack(pool_p), jnp.stack(conv_p),
            jnp.stack(ffn_p), jnp.stack(pool_s), jnp.stack(conv_s), jnp.stack(ffn_s))
```

```python
import functools

import jax
import jax.numpy as jnp
from jax import lax
from jax.experimental import pallas as pl
from jax.experimental.pallas import tpu as pltpu

LANES = 128
D_MODEL = 1024
POOL_WIDTH = 512
CONV_WIDTH = 512
POOL_WINDOWS = (2, 4, 8, 16)
POOL_GROUP = 128
POOL_HIST = 15
POOL_PAD = 16
HIST_PAD = 8
CONV_K = 3
N_MEM = 256
N_HEADS = 4
HEAD_DIM = 256
D_FF = 2816
FFN_CHUNK = 256
EPS = 1e-6
PAST_LEN = 2048

VMEM_LIMIT_BYTES = 56 * 1024 * 1024

_BF16 = jnp.bfloat16
_F32 = jnp.float32


def _rms(x, g):
    ms = jnp.mean(x * x, axis=-1, keepdims=True)
    return x * lax.rsqrt(ms + EPS) * g


def _dot(a, b):
    return jnp.dot(a, b, preferred_element_type=_F32)


def _tree_sum(xs):
    while len(xs) > 1:
        xs = [xs[i] + xs[i + 1] for i in range(0, len(xs) - 1, 2)] + ([xs[-1]] if len(xs) % 2 else [])
    return xs[0]


def _conv3_tile(buf, j, t, w_ref, b_ref, cols):
    taps = [buf[j, :, pl.ds(HIST_PAD - (CONV_K - 1) + k, t), :] * w_ref[k:k + 1, cols]
            for k in range(CONV_K)]
    return taps[0] + taps[1] + (taps[2] + b_ref[:, cols])


def _const_spec(shape, layer):
    nd = len(shape)
    return pl.BlockSpec((None,) + tuple(shape[1:]), lambda b, s: (layer,) + (0,) * (nd - 1),
                        pipeline_mode=pl.Buffered(1))


def _params():
    return pltpu.CompilerParams(dimension_semantics=("arbitrary", "arbitrary"),
                                vmem_limit_bytes=VMEM_LIMIT_BYTES)


def _mixer_kernel(x_ref, ph_ref, ch_ref, gpre_ref, win_ref, wpool_ref, pscale_ref, cw_ref, cb_ref,
                  wout_ref, gpost_ref, y_ref, pnew_ref, cnew_ref, pbuf, cbuf, *, start_pos):
    nb, t, d = x_ref.shape
    r = nb * t
    s = pl.program_id(1)
    n_tiles = POOL_WIDTH // LANES

    @pl.when(s == 0)
    def _():
        for j in range(n_tiles):
            cols = slice(j * LANES, (j + 1) * LANES)
            pbuf[j, :, 0:POOL_PAD, :] = ph_ref[:, :, cols]
            cbuf[j, :, 0:HIST_PAD, :] = ch_ref[:, :, cols]

    x = x_ref[...].reshape(r, d)
    h = _rms(x, gpre_ref[...]).astype(_BF16)
    proj = _dot(h, win_ref[...])
    u = proj[:, :POOL_WIDTH]
    b_gate = proj[:, POOL_WIDTH:POOL_WIDTH + CONV_WIDTH]
    cv = proj[:, POOL_WIDTH + CONV_WIDTH:POOL_WIDTH + 2 * CONV_WIDTH] * proj[:, POOL_WIDTH + 2 * CONV_WIDTH:]
    for j in range(n_tiles):
        cols = slice(j * LANES, (j + 1) * LANES)
        pbuf[j, :, POOL_PAD:, :] = u[:, cols].reshape(nb, t, LANES)
        cbuf[j, :, HIST_PAD:, :] = cv[:, cols].reshape(nb, t, LANES)

    pos = start_pos + s * t + lax.broadcasted_iota(jnp.int32, (nb, t, 1), 1)
    ys = []
    for g, w in enumerate(POOL_WINDOWS):
        cols = slice(g * POOL_GROUP, (g + 1) * POOL_GROUP)
        if w <= 8:
            win = _tree_sum([pbuf[g, :, pl.ds(POOL_PAD - i, t), :] for i in range(w)])
        else:
            s8 = _tree_sum([pbuf[g, :, pl.ds(8 - i, t + 8), :] for i in range(8)])
            win = s8[:, 8:, :] + s8[:, :t, :]
        inv_cnt = 1.0 / jnp.minimum(w, pos + 1).astype(_F32)
        dlt = ((win * inv_cnt).reshape(r, POOL_GROUP) - u[:, cols]).astype(_BF16)
        ys.append(_dot(dlt, wpool_ref[g]) * pscale_ref[:, cols])

    for j in range(n_tiles):
        cols = slice(j * LANES, (j + 1) * LANES)
        zc = _conv3_tile(cbuf, j, t, cw_ref, cb_ref, cols).reshape(r, LANES)
        ys.append(b_gate[:, cols] * zc)

    for j in range(n_tiles):
        cols = slice(j * LANES, (j + 1) * LANES)
        new_pool = pbuf[j, :, t:, :]
        new_conv = cbuf[j, :, t:, :]
        pnew_ref[:, :, cols] = new_pool
        cnew_ref[:, :, cols] = new_conv
        pbuf[j, :, 0:POOL_PAD, :] = new_pool
        cbuf[j, :, 0:HIST_PAD, :] = new_conv

    ycat = jnp.concatenate(ys, axis=-1).astype(_BF16)
    y = _dot(ycat, wout_ref[...])
    y_ref[...] = (x + _rms(y, gpost_ref[...])).reshape(nb, t, d)


def _mixer_call(x, pool_hist, conv_hist, hist_layer, layer, w, *, nb, t, start_pos):
    b, seq, d = x.shape
    grid = (b // nb, seq // t)
    tile = pl.BlockSpec((nb, t, d), lambda i, s: (i, s, 0))
    phs = pl.BlockSpec((None, nb, POOL_PAD, POOL_WIDTH), lambda i, s: (hist_layer, i, 0, 0))
    chs = pl.BlockSpec((None, nb, HIST_PAD, CONV_WIDTH), lambda i, s: (hist_layer, i, 0, 0))
    pos_ = pl.BlockSpec((nb, POOL_PAD, POOL_WIDTH), lambda i, s: (i, 0, 0))
    cos_ = pl.BlockSpec((nb, HIST_PAD, CONV_WIDTH), lambda i, s: (i, 0, 0))
    names = ('g_mix_pre', 'w_in', 'w_pool', 'pool_scale', 'conv_w', 'conv_b', 'w_out', 'g_mix_post')
    return pl.pallas_call(
        functools.partial(_mixer_kernel, start_pos=start_pos),
        grid=grid,
        in_specs=[tile, phs, chs] + [_const_spec(w[n].shape, layer) for n in names],
        out_specs=[tile, pos_, cos_],
        out_shape=[jax.ShapeDtypeStruct(x.shape, _F32),
                   jax.ShapeDtypeStruct((b, POOL_PAD, POOL_WIDTH), _F32),
                   jax.ShapeDtypeStruct((b, HIST_PAD, CONV_WIDTH), _F32)],
        scratch_shapes=[pltpu.VMEM((POOL_WIDTH // LANES, nb, POOL_PAD + t, LANES), _F32),
                        pltpu.VMEM((CONV_WIDTH // LANES, nb, HIST_PAD + t, LANES), _F32)],
        compiler_params=_params(),
        name="mixer",
    )(x, pool_hist, conv_hist, *[w[n] for n in names])


def _kv_copies(k_hbm, v_hbm, kbuf, vbuf, sem, kv_layer, step, slot, nb):
    cps = []
    for n in range(nb):
        for hd in range(N_HEADS):
            cps.append(pltpu.make_async_copy(k_hbm.at[kv_layer, step * nb + n, :, hd, :],
                                             kbuf.at[slot, n, hd], sem.at[slot, 0]))
            cps.append(pltpu.make_async_copy(v_hbm.at[kv_layer, step * nb + n, :, hd, :],
                                             vbuf.at[slot, n, hd], sem.at[slot, 1]))
    return cps


def _attn_kernel(x_ref, k_ref, v_ref, gpre_ref, wq_ref, wo_ref, gpost_ref, y_ref, *kv_scratch, kv_layer):
    nb, t, d = x_ref.shape
    r = nb * t

    if kv_scratch:
        kbuf, vbuf, sem = kv_scratch
        i = pl.program_id(0)
        slot = i % 2

        @pl.when(i == 0)
        def _():
            for c in _kv_copies(k_ref, v_ref, kbuf, vbuf, sem, kv_layer, 0, 0, nb):
                c.start()

        @pl.when(i + 1 < pl.num_programs(0))
        def _():
            for c in _kv_copies(k_ref, v_ref, kbuf, vbuf, sem, kv_layer, i + 1, 1 - slot, nb):
                c.start()

        for c in _kv_copies(k_ref, v_ref, kbuf, vbuf, sem, kv_layer, i, slot, nb):
            c.wait()

        def k_head(n, hd):
            return kbuf[slot, n, hd].astype(_BF16)

        def v_head(n, hd):
            return vbuf[slot, n, hd].astype(_BF16)
    else:
        def k_head(n, hd):
            return k_ref[n, :, hd * HEAD_DIM:(hd + 1) * HEAD_DIM]

        def v_head(n, hd):
            return v_ref[n, :, hd * HEAD_DIM:(hd + 1) * HEAD_DIM]

    x = x_ref[...].reshape(r, d)
    h = _rms(x, gpre_ref[...]).astype(_BF16)
    q = _dot(h, wq_ref[...]).astype(_BF16)
    rows = []
    for n in range(nb):
        heads = []
        for hd in range(N_HEADS):
            qh = q[n * t:(n + 1) * t, hd * HEAD_DIM:(hd + 1) * HEAD_DIM]
            sc = lax.dot_general(qh, k_head(n, hd), (((1,), (1,)), ((), ())),
                                 preferred_element_type=_F32)
            e = jnp.exp(sc - jnp.max(sc, axis=-1, keepdims=True))
            p = (e / jnp.sum(e, axis=-1, keepdims=True)).astype(_BF16)
            heads.append(_dot(p, v_head(n, hd)))
        rows.append(jnp.concatenate(heads, axis=-1))
    o = (rows[0] if nb == 1 else jnp.concatenate(rows, axis=0)).astype(_BF16)
    y = _dot(o, wo_ref[...])
    y_ref[...] = (x + _rms(y, gpost_ref[...])).reshape(nb, t, d)


def _attn_call(x, mem_k, mem_v, kv_layer, layer, w, *, nb, t):
    b, seq, d = x.shape
    grid = (b // nb, seq // t)
    tile = pl.BlockSpec((nb, t, d), lambda i, s: (i, s, 0))
    if mem_k.ndim == 4:
        kvs = pl.BlockSpec((None, nb, N_MEM, d), lambda i, s: (kv_layer, i, 0, 0))
        scratch = []
    else:
        assert seq == t, "the cache path prefetches per stream block along grid axis 0 only"
        kvs = pl.BlockSpec(memory_space=pl.ANY)
        scratch = [pltpu.VMEM((2, nb, N_HEADS, N_MEM, HEAD_DIM), _F32),
                   pltpu.VMEM((2, nb, N_HEADS, N_MEM, HEAD_DIM), _F32),
                   pltpu.SemaphoreType.DMA((2, 2))]
    names = ('g_attn_pre', 'w_q', 'w_o', 'g_attn_post')
    return pl.pallas_call(
        functools.partial(_attn_kernel, kv_layer=kv_layer),
        grid=grid,
        in_specs=[tile, kvs, kvs] + [_const_spec(w[n].shape, layer) for n in names],
        out_specs=tile,
        out_shape=jax.ShapeDtypeStruct(x.shape, _F32),
        scratch_shapes=scratch,
        compiler_params=_params(),
        name="attn",
    )(x, mem_k, mem_v, *[w[n] for n in names])


def _ffn_kernel(x_ref, fh_ref, gpre_ref, wup_ref, cw_ref, cb_ref, wdown_ref, gpost_ref,
                y_ref, fnew_ref, ubuf, hid):
    nb, t, d = x_ref.shape
    r = nb * t
    n_tiles = 2 * D_FF // LANES
    per_chunk = FFN_CHUNK // LANES

    @pl.when(pl.program_id(1) == 0)
    def _():
        for j in range(n_tiles):
            ubuf[j, :, 0:HIST_PAD, :] = fh_ref[:, :, j * LANES:(j + 1) * LANES]

    x = x_ref[...].reshape(r, d)
    h = _rms(x, gpre_ref[...]).astype(_BF16)

    def conv_cols(lo):
        up = _dot(h, wup_ref[:, lo:lo + FFN_CHUNK])
        outs = []
        for k in range(per_chunk):
            j = lo // LANES + k
            ubuf[j, :, HIST_PAD:, :] = up[:, k * LANES:(k + 1) * LANES].reshape(nb, t, LANES)
            cols = slice(j * LANES, (j + 1) * LANES)
            outs.append(_conv3_tile(ubuf, j, t, cw_ref, cb_ref, cols).reshape(r, LANES))
        return jnp.concatenate(outs, axis=-1)

    for c in range(D_FF // FFN_CHUNK):
        gate = conv_cols(c * FFN_CHUNK)
        value = conv_cols(D_FF + c * FFN_CHUNK)
        hid[:, c * FFN_CHUNK:(c + 1) * FFN_CHUNK] = (gate * jax.nn.sigmoid(gate) * value).astype(_BF16)

    for j in range(n_tiles):
        new_hist = ubuf[j, :, t:, :]
        fnew_ref[:, :, j * LANES:(j + 1) * LANES] = new_hist
        ubuf[j, :, 0:HIST_PAD, :] = new_hist

    y = _dot(hid[...], wdown_ref[...])
    y_ref[...] = (x + _rms(y, gpost_ref[...])).reshape(nb, t, d)


def _ffn_call(x, ffn_hist, hist_layer, layer, w, *, nb, t):
    b, seq, d = x.shape
    grid = (b // nb, seq // t)
    tile = pl.BlockSpec((nb, t, d), lambda i, s: (i, s, 0))
    fhs = pl.BlockSpec((None, nb, HIST_PAD, 2 * D_FF), lambda i, s: (hist_layer, i, 0, 0))
    fos = pl.BlockSpec((nb, HIST_PAD, 2 * D_FF), lambda i, s: (i, 0, 0))
    names = ('g_ffn_pre', 'w_up', 'ffn_conv_w', 'ffn_conv_b', 'w_down', 'g_ffn_post')
    return pl.pallas_call(
        _ffn_kernel,
        grid=grid,
        in_specs=[tile, fhs] + [_const_spec(w[n].shape, layer) for n in names],
        out_specs=[tile, fos],
        out_shape=[jax.ShapeDtypeStruct(x.shape, _F32),
                   jax.ShapeDtypeStruct((b, HIST_PAD, 2 * D_FF), _F32)],
        scratch_shapes=[pltpu.VMEM((2 * D_FF // LANES, nb, HIST_PAD + t, LANES), _F32),
                        pltpu.VMEM((nb * t, D_FF), _BF16)],
        compiler_params=_params(),
        name="ffn",
    )(x, ffn_hist, *[w[n] for n in names])


def _memkv_kernel(mem_ref, g_ref, wk_ref, wv_ref, k5_ref, v5_ref, kb_ref, vb_ref):
    nb, m, d = mem_ref.shape
    x = mem_ref[...].reshape(nb * m, d)
    h = _rms(x, g_ref[...]).astype(_BF16)
    for w_ref, o5_ref, ob_ref in ((wk_ref, k5_ref, kb_ref), (wv_ref, v5_ref, vb_ref)):
        kv = _dot(h, w_ref[...])
        ob_ref[...] = kv.astype(_BF16).reshape(nb, m, d)
        for n in range(nb):
            for hd in range(N_HEADS):
                o5_ref[n, :, hd, :] = kv[n * m:(n + 1) * m, hd * HEAD_DIM:(hd + 1) * HEAD_DIM]


def _memkv_call(mem, g_mem, w_k, w_v, *, nb):
    b, m, d = mem.shape
    depth = g_mem.shape[0]
    wspec = pl.BlockSpec((None, d, d), lambda l, i: (l, 0, 0))
    o5spec = pl.BlockSpec((None, nb, m, N_HEADS, HEAD_DIM), lambda l, i: (l, i, 0, 0, 0))
    obspec = pl.BlockSpec((None, nb, m, d), lambda l, i: (l, i, 0, 0))
    return pl.pallas_call(
        _memkv_kernel,
        grid=(depth, b // nb),
        in_specs=[pl.BlockSpec((nb, m, d), lambda l, i: (i, 0, 0)),
                  pl.BlockSpec((None, 1, d), lambda l, i: (l, 0, 0)), wspec, wspec],
        out_specs=[o5spec, o5spec, obspec, obspec],
        out_shape=[jax.ShapeDtypeStruct((depth, b, m, N_HEADS, HEAD_DIM), _F32)] * 2
        + [jax.ShapeDtypeStruct((depth, b, m, d), _BF16)] * 2,
        compiler_params=_params(),
        name="memkv",
    )(mem, g_mem, w_k, w_v)


def _pad_hist(h, pad):
    return jnp.pad(h, ((0, 0), (0, 0), (pad - h.shape[2], 0), (0, 0)))


def kernel(x_prompt, x_sample, mem_prompt, cache_mem_k, cache_mem_v, state_pool, state_conv, state_ffn_conv, g_mix_pre, g_mix_post, w_in, w_pool, pool_scale, conv_w, conv_b, w_out, g_attn_pre, g_attn_post, g_mem, w_q, w_k, w_v, w_o, g_ffn_pre, g_ffn_post, w_up, ffn_conv_w, ffn_conv_b, w_down):
    depth = w_in.shape[0]
    bp, seq, d = x_prompt.shape
    bs, dec_seq, _ = x_sample.shape

    def row(a):
        return a[:, None, :]

    w = {
        'g_mix_pre': row(g_mix_pre), 'g_mix_post': row(g_mix_post),
        'w_in': w_in.astype(_BF16), 'w_pool': w_pool.astype(_BF16), 'pool_scale': row(pool_scale),
        'conv_w': conv_w, 'conv_b': row(conv_b), 'w_out': w_out.astype(_BF16),
        'g_attn_pre': row(g_attn_pre), 'g_attn_post': row(g_attn_post),
        'w_q': (w_q.reshape(depth, d, d) * (HEAD_DIM ** -0.5)).astype(_BF16),
        'w_o': w_o.reshape(depth, d, d).astype(_BF16),
        'g_ffn_pre': row(g_ffn_pre), 'g_ffn_post': row(g_ffn_post),
        'w_up': w_up.astype(_BF16), 'ffn_conv_w': ffn_conv_w, 'ffn_conv_b': row(ffn_conv_b),
        'w_down': w_down.astype(_BF16),
    }

    mk_out, mv_out, mk_p, mv_p = _memkv_call(
        mem_prompt, row(g_mem), w_k.reshape(depth, d, d).astype(_BF16),
        w_v.reshape(depth, d, d).astype(_BF16), nb=2)
    mk_s, mv_s = cache_mem_k, cache_mem_v

    zero_pool = jnp.zeros((1, bp, POOL_PAD, POOL_WIDTH), _F32)
    zero_conv = jnp.zeros((1, bp, HIST_PAD, CONV_WIDTH), _F32)
    zero_ffn = jnp.zeros((1, bp, HIST_PAD, 2 * D_FF), _F32)
    pool_s_in = _pad_hist(state_pool, POOL_PAD)
    conv_s_in = _pad_hist(state_conv, HIST_PAD)
    ffn_s_in = _pad_hist(state_ffn_conv, HIST_PAD)

    tp = 512
    yp, ys = x_prompt, x_sample
    pool_p, conv_p, ffn_p, pool_s, conv_s, ffn_s = [], [], [], [], [], []
    for l in range(depth):
        yp, pn, cn = _mixer_call(yp, zero_pool, zero_conv, 0, l, w, nb=1, t=tp, start_pos=0)
        yp = _attn_call(yp, mk_p, mv_p, l, l, w, nb=1, t=tp)
        yp, fn = _ffn_call(yp, zero_ffn, 0, l, w, nb=1, t=tp)
        pool_p.append(pn[:, 1:])
        conv_p.append(cn[:, HIST_PAD - 2:])
        ffn_p.append(fn[:, HIST_PAD - 2:])
        ys, pn, cn = _mixer_call(ys, pool_s_in, conv_s_in, l, l, w, nb=8, t=dec_seq, start_pos=PAST_LEN)
        ys = _attn_call(ys, mk_s, mv_s, l, l, w, nb=4, t=dec_seq)
        ys, fn = _ffn_call(ys, ffn_s_in, l, l, w, nb=8, t=dec_seq)
        pool_s.append(pn[:, 1:])
        conv_s.append(cn[:, HIST_PAD - 2:])
        ffn_s.append(fn[:, HIST_PAD - 2:])

    return (yp, ys, mk_out, mv_out, jnp.stack(pool_p), jnp.stack(conv_p),
            jnp.stack(ffn_p), jnp.stack(pool_s), jnp.stack(conv_s), jnp.stack(ffn_s))
```

```python
import functools

import jax
import jax.numpy as jnp
from jax import lax
from jax.experimental import pallas as pl
from jax.experimental.pallas import tpu as pltpu

LANES = 128
D_MODEL = 1024
POOL_WIDTH = 512
CONV_WIDTH = 512
POOL_WINDOWS = (2, 4, 8, 16)
POOL_GROUP = 128
POOL_HIST = 15
POOL_PAD = 16
HIST_PAD = 8
CONV_K = 3
N_MEM = 256
N_HEADS = 4
HEAD_DIM = 256
D_FF = 2816
FFN_CHUNK = 256
EPS = 1e-6
PAST_LEN = 2048

VMEM_LIMIT_BYTES = 56 * 1024 * 1024

_BF16 = jnp.bfloat16
_F32 = jnp.float32


def _rms(x, g):
    ms = jnp.mean(x * x, axis=-1, keepdims=True)
    return x * lax.rsqrt(ms + EPS) * g


def _dot(a, b):
    return jnp.dot(a, b, preferred_element_type=_F32)


def _tree_sum(xs):
    while len(xs) > 1:
        xs = [xs[i] + xs[i + 1] for i in range(0, len(xs) - 1, 2)] + ([xs[-1]] if len(xs) % 2 else [])
    return xs[0]


def _conv3_tile(buf, j, t, w_ref, b_ref, cols):
    taps = [buf[j, :, pl.ds(HIST_PAD - (CONV_K - 1) + k, t), :] * w_ref[k:k + 1, cols]
            for k in range(CONV_K)]
    return taps[0] + taps[1] + (taps[2] + b_ref[:, cols])


def _const_spec(shape, layer):
    nd = len(shape)
    return pl.BlockSpec((None,) + tuple(shape[1:]), lambda b, s: (layer,) + (0,) * (nd - 1),
                        pipeline_mode=pl.Buffered(1))


def _params():
    return pltpu.CompilerParams(dimension_semantics=("arbitrary", "arbitrary"),
                                vmem_limit_bytes=VMEM_LIMIT_BYTES)


def _mixer_kernel(x_ref, ph_ref, ch_ref, gpre_ref, win_ref, wpool_ref, pscale_ref, cw_ref, cb_ref,
                  wout_ref, gpost_ref, y_ref, pnew_ref, cnew_ref, pbuf, cbuf, *, start_pos):
    nb, t, d = x_ref.shape
    r = nb * t
    s = pl.program_id(1)
    n_tiles = POOL_WIDTH // LANES

    @pl.when(s == 0)
    def _():
        for j in range(n_tiles):
            cols = slice(j * LANES, (j + 1) * LANES)
            pbuf[j, :, 0:POOL_PAD, :] = ph_ref[:, :, cols]
            cbuf[j, :, 0:HIST_PAD, :] = ch_ref[:, :, cols]

    x = x_ref[...].reshape(r, d)
    h = _rms(x, gpre_ref[...]).astype(_BF16)
    proj = _dot(h, win_ref[...])
    u = proj[:, :POOL_WIDTH]
    b_gate = proj[:, POOL_WIDTH:POOL_WIDTH + CONV_WIDTH]
    cv = proj[:, POOL_WIDTH + CONV_WIDTH:POOL_WIDTH + 2 * CONV_WIDTH] * proj[:, POOL_WIDTH + 2 * CONV_WIDTH:]
    for j in range(n_tiles):
        cols = slice(j * LANES, (j + 1) * LANES)
        pbuf[j, :, POOL_PAD:, :] = u[:, cols].reshape(nb, t, LANES)
        cbuf[j, :, HIST_PAD:, :] = cv[:, cols].reshape(nb, t, LANES)

    pos = start_pos + s * t + lax.broadcasted_iota(jnp.int32, (nb, t, 1), 1)
    ys = []
    for g, w in enumerate(POOL_WINDOWS):
        cols = slice(g * POOL_GROUP, (g + 1) * POOL_GROUP)
        if w <= 8:
            win = _tree_sum([pbuf[g, :, pl.ds(POOL_PAD - i, t), :] for i in range(w)])
        else:
            s8 = _tree_sum([pbuf[g, :, pl.ds(8 - i, t + 8), :] for i in range(8)])
            win = s8[:, 8:, :] + s8[:, :t, :]
        inv_cnt = 1.0 / jnp.minimum(w, pos + 1).astype(_F32)
        dlt = ((win * inv_cnt).reshape(r, POOL_GROUP) - u[:, cols]).astype(_BF16)
        ys.append(_dot(dlt, wpool_ref[g]) * pscale_ref[:, cols])

    for j in range(n_tiles):
        cols = slice(j * LANES, (j + 1) * LANES)
        zc = _conv3_tile(cbuf, j, t, cw_ref, cb_ref, cols).reshape(r, LANES)
        ys.append(b_gate[:, cols] * zc)

    for j in range(n_tiles):
        cols = slice(j * LANES, (j + 1) * LANES)
        new_pool = pbuf[j, :, t:, :]
        new_conv = cbuf[j, :, t:, :]
        pnew_ref[:, :, cols] = new_pool
        cnew_ref[:, :, cols] = new_conv
        pbuf[j, :, 0:POOL_PAD, :] = new_pool
        cbuf[j, :, 0:HIST_PAD, :] = new_conv

    ycat = jnp.concatenate(ys, axis=-1).astype(_BF16)
    y = _dot(ycat, wout_ref[...])
    y_ref[...] = (x + _rms(y, gpost_ref[...])).reshape(nb, t, d)


def _mixer_call(x, pool_hist, conv_hist, hist_layer, layer, w, *, nb, t, start_pos):
    b, seq, d = x.shape
    grid = (b // nb, seq // t)
    tile = pl.BlockSpec((nb, t, d), lambda i, s: (i, s, 0))
    phs = pl.BlockSpec((None, nb, POOL_PAD, POOL_WIDTH), lambda i, s: (hist_layer, i, 0, 0))
    chs = pl.BlockSpec((None, nb, HIST_PAD, CONV_WIDTH), lambda i, s: (hist_layer, i, 0, 0))
    pos_ = pl.BlockSpec((nb, POOL_PAD, POOL_WIDTH), lambda i, s: (i, 0, 0))
    cos_ = pl.BlockSpec((nb, HIST_PAD, CONV_WIDTH), lambda i, s: (i, 0, 0))
    names = ('g_mix_pre', 'w_in', 'w_pool', 'pool_scale', 'conv_w', 'conv_b', 'w_out', 'g_mix_post')
    return pl.pallas_call(
        functools.partial(_mixer_kernel, start_pos=start_pos),
        grid=grid,
        in_specs=[tile, phs, chs] + [_const_spec(w[n].shape, layer) for n in names],
        out_specs=[tile, pos_, cos_],
        out_shape=[jax.ShapeDtypeStruct(x.shape, _F32),
                   jax.ShapeDtypeStruct((b, POOL_PAD, POOL_WIDTH), _F32),
                   jax.ShapeDtypeStruct((b, HIST_PAD, CONV_WIDTH), _F32)],
        scratch_shapes=[pltpu.VMEM((POOL_WIDTH // LANES, nb, POOL_PAD + t, LANES), _F32),
                        pltpu.VMEM((CONV_WIDTH // LANES, nb, HIST_PAD + t, LANES), _F32)],
        compiler_params=_params(),
        name="mixer",
    )(x, pool_hist, conv_hist, *[w[n] for n in names])


def _kv_copies(k_hbm, v_hbm, kbuf, vbuf, sem, kv_layer, step, slot, nb):
    cps = []
    for n in range(nb):
        for hd in range(N_HEADS):
            cps.append(pltpu.make_async_copy(k_hbm.at[kv_layer, step * nb + n, :, hd, :],
                                             kbuf.at[slot, n, hd], sem.at[slot, 0]))
            cps.append(pltpu.make_async_copy(v_hbm.at[kv_layer, step * nb + n, :, hd, :],
                                             vbuf.at[slot, n, hd], sem.at[slot, 1]))
    return cps


def _attn_kernel(x_ref, k_ref, v_ref, gpre_ref, wq_ref, wo_ref, gpost_ref, y_ref, *kv_scratch, kv_layer):
    nb, t, d = x_ref.shape
    r = nb * t

    if kv_scratch:
        kbuf, vbuf, sem = kv_scratch
        i = pl.program_id(0)
        slot = i % 2

        @pl.when(i == 0)
        def _():
            for c in _kv_copies(k_ref, v_ref, kbuf, vbuf, sem, kv_layer, 0, 0, nb):
                c.start()

        @pl.when(i + 1 < pl.num_programs(0))
        def _():
            for c in _kv_copies(k_ref, v_ref, kbuf, vbuf, sem, kv_layer, i + 1, 1 - slot, nb):
                c.start()

        for c in _kv_copies(k_ref, v_ref, kbuf, vbuf, sem, kv_layer, i, slot, nb):
            c.wait()

        def k_head(n, hd):
            return kbuf[slot, n, hd].astype(_BF16)

        def v_head(n, hd):
            return vbuf[slot, n, hd].astype(_BF16)
    else:
        def k_head(n, hd):
            return k_ref[n, :, hd * HEAD_DIM:(hd + 1) * HEAD_DIM]

        def v_head(n, hd):
            return v_ref[n, :, hd * HEAD_DIM:(hd + 1) * HEAD_DIM]

    x = x_ref[...].reshape(r, d)
    h = _rms(x, gpre_ref[...]).astype(_BF16)
    q = _dot(h, wq_ref[...]).astype(_BF16)
    scores = {}
    for n in range(nb):
        for hd in range(N_HEADS):
            qh = q[n * t:(n + 1) * t, hd * HEAD_DIM:(hd + 1) * HEAD_DIM]
            scores[n, hd] = lax.dot_general(qh, k_head(n, hd), (((1,), (1,)), ((), ())),
                                            preferred_element_type=_F32)
    rows = []
    for n in range(nb):
        heads = []
        for hd in range(N_HEADS):
            sc = scores[n, hd]
            e = jnp.exp(sc - jnp.max(sc, axis=-1, keepdims=True))
            p = (e / jnp.sum(e, axis=-1, keepdims=True)).astype(_BF16)
            heads.append(_dot(p, v_head(n, hd)))
        rows.append(jnp.concatenate(heads, axis=-1))
    o = (rows[0] if nb == 1 else jnp.concatenate(rows, axis=0)).astype(_BF16)
    y = _dot(o, wo_ref[...])
    y_ref[...] = (x + _rms(y, gpost_ref[...])).reshape(nb, t, d)


def _attn_call(x, mem_k, mem_v, kv_layer, layer, w, *, nb, t):
    b, seq, d = x.shape
    grid = (b // nb, seq // t)
    tile = pl.BlockSpec((nb, t, d), lambda i, s: (i, s, 0))
    if mem_k.ndim == 4:
        kvs = pl.BlockSpec((None, nb, N_MEM, d), lambda i, s: (kv_layer, i, 0, 0))
        scratch = []
    else:
        assert seq == t, "the cache path prefetches per stream block along grid axis 0 only"
        kvs = pl.BlockSpec(memory_space=pl.ANY)
        scratch = [pltpu.VMEM((2, nb, N_HEADS, N_MEM, HEAD_DIM), _F32),
                   pltpu.VMEM((2, nb, N_HEADS, N_MEM, HEAD_DIM), _F32),
                   pltpu.SemaphoreType.DMA((2, 2))]
    names = ('g_attn_pre', 'w_q', 'w_o', 'g_attn_post')
    return pl.pallas_call(
        functools.partial(_attn_kernel, kv_layer=kv_layer),
        grid=grid,
        in_specs=[tile, kvs, kvs] + [_const_spec(w[n].shape, layer) for n in names],
        out_specs=tile,
        out_shape=jax.ShapeDtypeStruct(x.shape, _F32),
        scratch_shapes=scratch,
        compiler_params=_params(),
        name="attn",
    )(x, mem_k, mem_v, *[w[n] for n in names])


def _ffn_kernel(x_ref, fh_ref, gpre_ref, wup_ref, cw_ref, cb_ref, wdown_ref, gpost_ref,
                y_ref, fnew_ref, ubuf, hid):
    nb, t, d = x_ref.shape
    r = nb * t
    n_tiles = 2 * D_FF // LANES
    per_chunk = FFN_CHUNK // LANES

    @pl.when(pl.program_id(1) == 0)
    def _():
        for j in range(n_tiles):
            ubuf[j, :, 0:HIST_PAD, :] = fh_ref[:, :, j * LANES:(j + 1) * LANES]

    x = x_ref[...].reshape(r, d)
    h = _rms(x, gpre_ref[...]).astype(_BF16)

    def conv_cols(lo):
        up = _dot(h, wup_ref[:, lo:lo + FFN_CHUNK])
        outs = []
        for k in range(per_chunk):
            j = lo // LANES + k
            ubuf[j, :, HIST_PAD:, :] = up[:, k * LANES:(k + 1) * LANES].reshape(nb, t, LANES)
            cols = slice(j * LANES, (j + 1) * LANES)
            outs.append(_conv3_tile(ubuf, j, t, cw_ref, cb_ref, cols).reshape(r, LANES))
        return jnp.concatenate(outs, axis=-1)

    for c in range(D_FF // FFN_CHUNK):
        gate = conv_cols(c * FFN_CHUNK)
        value = conv_cols(D_FF + c * FFN_CHUNK)
        hid[:, c * FFN_CHUNK:(c + 1) * FFN_CHUNK] = (gate * jax.nn.sigmoid(gate) * value).astype(_BF16)

    for j in range(n_tiles):
        new_hist = ubuf[j, :, t:, :]
        fnew_ref[:, :, j * LANES:(j + 1) * LANES] = new_hist
        ubuf[j, :, 0:HIST_PAD, :] = new_hist

    y = _dot(hid[...], wdown_ref[...])
    y_ref[...] = (x + _rms(y, gpost_ref[...])).reshape(nb, t, d)


def _ffn_call(x, ffn_hist, hist_layer, layer, w, *, nb, t):
    b, seq, d = x.shape
    grid = (b // nb, seq // t)
    tile = pl.BlockSpec((nb, t, d), lambda i, s: (i, s, 0))
    fhs = pl.BlockSpec((None, nb, HIST_PAD, 2 * D_FF), lambda i, s: (hist_layer, i, 0, 0))
    fos = pl.BlockSpec((nb, HIST_PAD, 2 * D_FF), lambda i, s: (i, 0, 0))
    names = ('g_ffn_pre', 'w_up', 'ffn_conv_w', 'ffn_conv_b', 'w_down', 'g_ffn_post')
    return pl.pallas_call(
        _ffn_kernel,
        grid=grid,
        in_specs=[tile, fhs] + [_const_spec(w[n].shape, layer) for n in names],
        out_specs=[tile, fos],
        out_shape=[jax.ShapeDtypeStruct(x.shape, _F32),
                   jax.ShapeDtypeStruct((b, HIST_PAD, 2 * D_FF), _F32)],
        scratch_shapes=[pltpu.VMEM((2 * D_FF // LANES, nb, HIST_PAD + t, LANES), _F32),
                        pltpu.VMEM((nb * t, D_FF), _BF16)],
        compiler_params=_params(),
        name="ffn",
    )(x, ffn_hist, *[w[n] for n in names])


def _memkv_kernel(mem_ref, g_ref, wk_ref, wv_ref, k5_ref, v5_ref, kb_ref, vb_ref):
    nb, m, d = mem_ref.shape
    x = mem_ref[...].reshape(nb * m, d)
    h = _rms(x, g_ref[...]).astype(_BF16)
    for w_ref, o5_ref, ob_ref in ((wk_ref, k5_ref, kb_ref), (wv_ref, v5_ref, vb_ref)):
        kv = _dot(h, w_ref[...])
        ob_ref[...] = kv.astype(_BF16).reshape(nb, m, d)
        for n in range(nb):
            for hd in range(N_HEADS):
                o5_ref[n, :, hd, :] = kv[n * m:(n + 1) * m, hd * HEAD_DIM:(hd + 1) * HEAD_DIM]


def _memkv_call(mem, g_mem, w_k, w_v, *, nb):
    b, m, d = mem.shape
    depth = g_mem.shape[0]
    wspec = pl.BlockSpec((None, d, d), lambda l, i: (l, 0, 0))
    o5spec = pl.BlockSpec((None, nb, m, N_HEADS, HEAD_DIM), lambda l, i: (l, i, 0, 0, 0))
    obspec = pl.BlockSpec((None, nb, m, d), lambda l, i: (l, i, 0, 0))
    return pl.pallas_call(
        _memkv_kernel,
        grid=(depth, b // nb),
        in_specs=[pl.BlockSpec((nb, m, d), lambda l, i: (i, 0, 0)),
                  pl.BlockSpec((None, 1, d), lambda l, i: (l, 0, 0)), wspec, wspec],
        out_specs=[o5spec, o5spec, obspec, obspec],
        out_shape=[jax.ShapeDtypeStruct((depth, b, m, N_HEADS, HEAD_DIM), _F32)] * 2
        + [jax.ShapeDtypeStruct((depth, b, m, d), _BF16)] * 2,
        compiler_params=_params(),
        name="memkv",
    )(mem, g_mem, w_k, w_v)


def _pad_hist(h, pad):
    return jnp.pad(h, ((0, 0), (0, 0), (pad - h.shape[2], 0), (0, 0)))


def kernel(x_prompt, x_sample, mem_prompt, cache_mem_k, cache_mem_v, state_pool, state_conv, state_ffn_conv, g_mix_pre, g_mix_post, w_in, w_pool, pool_scale, conv_w, conv_b, w_out, g_attn_pre, g_attn_post, g_mem, w_q, w_k, w_v, w_o, g_ffn_pre, g_ffn_post, w_up, ffn_conv_w, ffn_conv_b, w_down):
    depth = w_in.shape[0]
    bp, seq, d = x_prompt.shape
    bs, dec_seq, _ = x_sample.shape

    def row(a):
        return a[:, None, :]

    w = {
        'g_mix_pre': row(g_mix_pre), 'g_mix_post': row(g_mix_post),
        'w_in': w_in.astype(_BF16), 'w_pool': w_pool.astype(_BF16), 'pool_scale': row(pool_scale),
        'conv_w': conv_w, 'conv_b': row(conv_b), 'w_out': w_out.astype(_BF16),
        'g_attn_pre': row(g_attn_pre), 'g_attn_post': row(g_attn_post),
        'w_q': (w_q.reshape(depth, d, d) * (HEAD_DIM ** -0.5)).astype(_BF16),
        'w_o': w_o.reshape(depth, d, d).astype(_BF16),
        'g_ffn_pre': row(g_ffn_pre), 'g_ffn_post': row(g_ffn_post),
        'w_up': w_up.astype(_BF16), 'ffn_conv_w': ffn_conv_w, 'ffn_conv_b': row(ffn_conv_b),
        'w_down': w_down.astype(_BF16),
    }

    mk_out, mv_out, mk_p, mv_p = _memkv_call(
        mem_prompt, row(g_mem), w_k.reshape(depth, d, d).astype(_BF16),
        w_v.reshape(depth, d, d).astype(_BF16), nb=2)
    mk_s, mv_s = cache_mem_k, cache_mem_v

    zero_pool = jnp.zeros((1, bp, POOL_PAD, POOL_WIDTH), _F32)
    zero_conv = jnp.zeros((1, bp, HIST_PAD, CONV_WIDTH), _F32)
    zero_ffn = jnp.zeros((1, bp, HIST_PAD, 2 * D_FF), _F32)
    pool_s_in = _pad_hist(state_pool, POOL_PAD)
    conv_s_in = _pad_hist(state_conv, HIST_PAD)
    ffn_s_in = _pad_hist(state_ffn_conv, HIST_PAD)

    tp = 512
    yp, ys = x_prompt, x_sample
    pool_p, conv_p, ffn_p, pool_s, conv_s, ffn_s = [], [], [], [], [], []
    for l in range(depth):
        yp, pn, cn = _mixer_call(yp, zero_pool, zero_conv, 0, l, w, nb=1, t=tp, start_pos=0)
        yp = _attn_call(yp, mk_p, mv_p, l, l, w, nb=1, t=tp)
        yp, fn = _ffn_call(yp, zero_ffn, 0, l, w, nb=1, t=tp)
        pool_p.append(pn[:, 1:])
        conv_p.append(cn[:, HIST_PAD - 2:])
        ffn_p.append(fn[:, HIST_PAD - 2:])
        ys, pn, cn = _mixer_call(ys, pool_s_in, conv_s_in, l, l, w, nb=8, t=dec_seq, start_pos=PAST_LEN)
        ys = _attn_call(ys, mk_s, mv_s, l, l, w, nb=4, t=dec_seq)
        ys, fn = _ffn_call(ys, ffn_s_in, l, l, w, nb=8, t=dec_seq)
        pool_s.append(pn[:, 1:])
        conv_s.append(cn[:, HIST_PAD - 2:])
        ffn_s.append(fn[:, HIST_PAD - 2:])

    return (yp, ys, mk_out, mv_out, jnp.stack(pool_p), jnp.stack(conv_p),
            jnp.stack(ffn_p), jnp.stack(pool_s), jnp.stack(conv_s), jnp.stack(ffn_s))
```

```python
import functools

import jax
import jax.numpy as jnp
from jax import lax
from jax.experimental import pallas as pl
from jax.experimental.pallas import tpu as pltpu

LANES = 128
D_MODEL = 1024
POOL_WIDTH = 512
CONV_WIDTH = 512
POOL_WINDOWS = (2, 4, 8, 16)
POOL_GROUP = 128
POOL_HIST = 15
POOL_PAD = 16
HIST_PAD = 8
CONV_K = 3
N_MEM = 256
N_HEADS = 4
HEAD_DIM = 256
D_FF = 2816
FFN_CHUNK = 256
EPS = 1e-6
PAST_LEN = 2048

VMEM_LIMIT_BYTES = 56 * 1024 * 1024

_BF16 = jnp.bfloat16
_F32 = jnp.float32


def _rms(x, g):
    ms = jnp.mean(x * x, axis=-1, keepdims=True)
    return x * lax.rsqrt(ms + EPS) * g


def _dot(a, b):
    return jnp.dot(a, b, preferred_element_type=_F32)


def _tree_sum(xs):
    while len(xs) > 1:
        xs = [xs[i] + xs[i + 1] for i in range(0, len(xs) - 1, 2)] + ([xs[-1]] if len(xs) % 2 else [])
    return xs[0]


def _conv3_tile(buf, j, t, w_ref, b_ref, cols):
    taps = [buf[j, :, pl.ds(HIST_PAD - (CONV_K - 1) + k, t), :] * w_ref[k:k + 1, cols]
            for k in range(CONV_K)]
    return taps[0] + taps[1] + (taps[2] + b_ref[:, cols])


def _const_spec(shape, layer):
    nd = len(shape)
    return pl.BlockSpec((None,) + tuple(shape[1:]), lambda b, s: (layer,) + (0,) * (nd - 1),
                        pipeline_mode=pl.Buffered(1))


def _params():
    return pltpu.CompilerParams(dimension_semantics=("arbitrary", "arbitrary"),
                                vmem_limit_bytes=VMEM_LIMIT_BYTES)


def _mixer_kernel(x_ref, *refs, start_pos, has_hist):
    ph_ref, ch_ref = refs[:2] if has_hist else (None, None)
    (gpre_ref, win_ref, wpool_ref, pscale_ref, cw_ref, cb_ref, wout_ref, gpost_ref,
     y_ref, pnew_ref, cnew_ref, pbuf, cbuf) = refs[2:] if has_hist else refs
    nb, t, d = x_ref.shape
    r = nb * t
    s = pl.program_id(1)
    n_tiles = POOL_WIDTH // LANES

    @pl.when(s == 0)
    def _():
        for j in range(n_tiles):
            cols = slice(j * LANES, (j + 1) * LANES)
            if has_hist:
                pbuf[j, :, POOL_PAD - POOL_HIST:POOL_PAD, :] = ph_ref[:, :, cols]
                cbuf[j, :, HIST_PAD - (CONV_K - 1):HIST_PAD, :] = ch_ref[:, :, cols]
            else:
                pbuf[j, :, 0:POOL_PAD, :] = jnp.zeros((nb, POOL_PAD, LANES), _F32)
                cbuf[j, :, 0:HIST_PAD, :] = jnp.zeros((nb, HIST_PAD, LANES), _F32)

    x = x_ref[...].reshape(r, d)
    h = _rms(x, gpre_ref[...]).astype(_BF16)
    proj = _dot(h, win_ref[...])
    u = proj[:, :POOL_WIDTH]
    b_gate = proj[:, POOL_WIDTH:POOL_WIDTH + CONV_WIDTH]
    cv = proj[:, POOL_WIDTH + CONV_WIDTH:POOL_WIDTH + 2 * CONV_WIDTH] * proj[:, POOL_WIDTH + 2 * CONV_WIDTH:]
    for j in range(n_tiles):
        cols = slice(j * LANES, (j + 1) * LANES)
        pbuf[j, :, POOL_PAD:, :] = u[:, cols].reshape(nb, t, LANES)
        cbuf[j, :, HIST_PAD:, :] = cv[:, cols].reshape(nb, t, LANES)

    pos = start_pos + s * t + lax.broadcasted_iota(jnp.int32, (nb, t, 1), 1)
    ys = []
    for g, w in enumerate(POOL_WINDOWS):
        cols = slice(g * POOL_GROUP, (g + 1) * POOL_GROUP)
        if w <= 8:
            win = _tree_sum([pbuf[g, :, pl.ds(POOL_PAD - i, t), :] for i in range(w)])
        else:
            s8 = _tree_sum([pbuf[g, :, pl.ds(8 - i, t + 8), :] for i in range(8)])
            win = s8[:, 8:, :] + s8[:, :t, :]
        inv_cnt = 1.0 / jnp.minimum(w, pos + 1).astype(_F32)
        dlt = ((win * inv_cnt).reshape(r, POOL_GROUP) - u[:, cols]).astype(_BF16)
        ys.append(_dot(dlt, wpool_ref[g]) * pscale_ref[:, cols])

    for j in range(n_tiles):
        cols = slice(j * LANES, (j + 1) * LANES)
        zc = _conv3_tile(cbuf, j, t, cw_ref, cb_ref, cols).reshape(r, LANES)
        ys.append(b_gate[:, cols] * zc)

    for j in range(n_tiles):
        cols = slice(j * LANES, (j + 1) * LANES)
        pnew_ref[:, :, cols] = pbuf[j, :, pl.ds(POOL_PAD + t - POOL_HIST, POOL_HIST), :]
        cnew_ref[:, :, cols] = cbuf[j, :, pl.ds(HIST_PAD + t - (CONV_K - 1), CONV_K - 1), :]
        pbuf[j, :, 0:POOL_PAD, :] = pbuf[j, :, t:, :]
        cbuf[j, :, 0:HIST_PAD, :] = cbuf[j, :, t:, :]

    ycat = jnp.concatenate(ys, axis=-1).astype(_BF16)
    y = _dot(ycat, wout_ref[...])
    y_ref[...] = (x + _rms(y, gpost_ref[...])).reshape(nb, t, d)


def _mixer_call(x, hists, layer, w, *, nb, t, start_pos):
    b, seq, d = x.shape
    grid = (b // nb, seq // t)
    tile = pl.BlockSpec((nb, t, d), lambda i, s: (i, s, 0))
    hist_specs = [] if hists is None else [
        pl.BlockSpec((None, nb, POOL_HIST, POOL_WIDTH), lambda i, s: (layer, i, 0, 0)),
        pl.BlockSpec((None, nb, CONV_K - 1, CONV_WIDTH), lambda i, s: (layer, i, 0, 0))]
    pos_ = pl.BlockSpec((nb, POOL_HIST, POOL_WIDTH), lambda i, s: (i, 0, 0))
    cos_ = pl.BlockSpec((nb, CONV_K - 1, CONV_WIDTH), lambda i, s: (i, 0, 0))
    names = ('g_mix_pre', 'w_in', 'w_pool', 'pool_scale', 'conv_w', 'conv_b', 'w_out', 'g_mix_post')
    return pl.pallas_call(
        functools.partial(_mixer_kernel, start_pos=start_pos, has_hist=hists is not None),
        grid=grid,
        in_specs=[tile] + hist_specs + [_const_spec(w[n].shape, layer) for n in names],
        out_specs=[tile, pos_, cos_],
        out_shape=[jax.ShapeDtypeStruct(x.shape, _F32),
                   jax.ShapeDtypeStruct((b, POOL_HIST, POOL_WIDTH), _F32),
                   jax.ShapeDtypeStruct((b, CONV_K - 1, CONV_WIDTH), _F32)],
        scratch_shapes=[pltpu.VMEM((POOL_WIDTH // LANES, nb, POOL_PAD + t, LANES), _F32),
                        pltpu.VMEM((CONV_WIDTH // LANES, nb, HIST_PAD + t, LANES), _F32)],
        compiler_params=_params(),
        name="mixer",
    )(x, *(hists or ()), *[w[n] for n in names])


def _kv_copies(k_hbm, v_hbm, kbuf, vbuf, sem, kv_layer, step, slot, nb):
    cps = []
    for n in range(nb):
        for hd in range(N_HEADS):
            cps.append(pltpu.make_async_copy(k_hbm.at[kv_layer, step * nb + n, :, hd, :],
                                             kbuf.at[slot, n, hd], sem.at[slot, 0]))
            cps.append(pltpu.make_async_copy(v_hbm.at[kv_layer, step * nb + n, :, hd, :],
                                             vbuf.at[slot, n, hd], sem.at[slot, 1]))
    return cps


def _attn_kernel(x_ref, k_ref, v_ref, gpre_ref, wq_ref, wo_ref, gpost_ref, y_ref, *kv_scratch, kv_layer):
    nb, t, d = x_ref.shape
    r = nb * t

    if kv_scratch:
        kbuf, vbuf, sem = kv_scratch
        i = pl.program_id(0)
        slot = i % 2

        @pl.when(i == 0)
        def _():
            for c in _kv_copies(k_ref, v_ref, kbuf, vbuf, sem, kv_layer, 0, 0, nb):
                c.start()

        @pl.when(i + 1 < pl.num_programs(0))
        def _():
            for c in _kv_copies(k_ref, v_ref, kbuf, vbuf, sem, kv_layer, i + 1, 1 - slot, nb):
                c.start()

        for c in _kv_copies(k_ref, v_ref, kbuf, vbuf, sem, kv_layer, i, slot, nb):
            c.wait()

        def k_head(n, hd):
            return kbuf[slot, n, hd].astype(_BF16)

        def v_head(n, hd):
            return vbuf[slot, n, hd].astype(_BF16)
    else:
        def k_head(n, hd):
            return k_ref[n, :, hd * HEAD_DIM:(hd + 1) * HEAD_DIM]

        def v_head(n, hd):
            return v_ref[n, :, hd * HEAD_DIM:(hd + 1) * HEAD_DIM]

    x = x_ref[...].reshape(r, d)
    h = _rms(x, gpre_ref[...]).astype(_BF16)
    q = _dot(h, wq_ref[...]).astype(_BF16)
    scores = {}
    for n in range(nb):
        for hd in range(N_HEADS):
            qh = q[n * t:(n + 1) * t, hd * HEAD_DIM:(hd + 1) * HEAD_DIM]
            scores[n, hd] = lax.dot_general(qh, k_head(n, hd), (((1,), (1,)), ((), ())),
                                            preferred_element_type=_F32)
    rows = []
    for n in range(nb):
        heads = []
        for hd in range(N_HEADS):
            sc = scores[n, hd]
            e = jnp.exp(sc - jnp.max(sc, axis=-1, keepdims=True))
            p = (e / jnp.sum(e, axis=-1, keepdims=True)).astype(_BF16)
            heads.append(_dot(p, v_head(n, hd)))
        rows.append(jnp.concatenate(heads, axis=-1))
    o = (rows[0] if nb == 1 else jnp.concatenate(rows, axis=0)).astype(_BF16)
    y = _dot(o, wo_ref[...])
    y_ref[...] = (x + _rms(y, gpost_ref[...])).reshape(nb, t, d)


def _attn_call(x, mem_k, mem_v, kv_layer, layer, w, *, nb, t):
    b, seq, d = x.shape
    grid = (b // nb, seq // t)
    tile = pl.BlockSpec((nb, t, d), lambda i, s: (i, s, 0))
    if mem_k.ndim == 4:
        kvs = pl.BlockSpec((None, nb, N_MEM, d), lambda i, s: (kv_layer, i, 0, 0))
        scratch = []
    else:
        assert seq == t, "the cache path prefetches per stream block along grid axis 0 only"
        kvs = pl.BlockSpec(memory_space=pl.ANY)
        scratch = [pltpu.VMEM((2, nb, N_HEADS, N_MEM, HEAD_DIM), _F32),
                   pltpu.VMEM((2, nb, N_HEADS, N_MEM, HEAD_DIM), _F32),
                   pltpu.SemaphoreType.DMA((2, 2))]
    names = ('g_attn_pre', 'w_q', 'w_o', 'g_attn_post')
    return pl.pallas_call(
        functools.partial(_attn_kernel, kv_layer=kv_layer),
        grid=grid,
        in_specs=[tile, kvs, kvs] + [_const_spec(w[n].shape, layer) for n in names],
        out_specs=tile,
        out_shape=jax.ShapeDtypeStruct(x.shape, _F32),
        scratch_shapes=scratch,
        compiler_params=_params(),
        name="attn",
    )(x, mem_k, mem_v, *[w[n] for n in names])


def _ffn_kernel(x_ref, *refs, has_hist):
    fh_ref = refs[0] if has_hist else None
    (gpre_ref, wup_ref, cw_ref, cb_ref, wdown_ref, gpost_ref,
     y_ref, fnew_ref, ubuf, hid) = refs[1:] if has_hist else refs
    nb, t, d = x_ref.shape
    r = nb * t
    n_tiles = 2 * D_FF // LANES
    per_chunk = FFN_CHUNK // LANES

    @pl.when(pl.program_id(1) == 0)
    def _():
        for j in range(n_tiles):
            if has_hist:
                ubuf[j, :, HIST_PAD - (CONV_K - 1):HIST_PAD, :] = fh_ref[:, :, j * LANES:(j + 1) * LANES]
            else:
                ubuf[j, :, 0:HIST_PAD, :] = jnp.zeros((nb, HIST_PAD, LANES), _F32)

    x = x_ref[...].reshape(r, d)
    h = _rms(x, gpre_ref[...]).astype(_BF16)

    def conv_cols(lo):
        up = _dot(h, wup_ref[:, lo:lo + FFN_CHUNK])
        outs = []
        for k in range(per_chunk):
            j = lo // LANES + k
            ubuf[j, :, HIST_PAD:, :] = up[:, k * LANES:(k + 1) * LANES].reshape(nb, t, LANES)
            cols = slice(j * LANES, (j + 1) * LANES)
            outs.append(_conv3_tile(ubuf, j, t, cw_ref, cb_ref, cols).reshape(r, LANES))
        return jnp.concatenate(outs, axis=-1)

    for c in range(D_FF // FFN_CHUNK):
        gate = conv_cols(c * FFN_CHUNK)
        value = conv_cols(D_FF + c * FFN_CHUNK)
        hid[:, c * FFN_CHUNK:(c + 1) * FFN_CHUNK] = (gate * jax.nn.sigmoid(gate) * value).astype(_BF16)

    for j in range(n_tiles):
        fnew_ref[:, :, j * LANES:(j + 1) * LANES] = ubuf[j, :, pl.ds(HIST_PAD + t - (CONV_K - 1), CONV_K - 1), :]
        ubuf[j, :, 0:HIST_PAD, :] = ubuf[j, :, t:, :]

    y = _dot(hid[...], wdown_ref[...])
    y_ref[...] = (x + _rms(y, gpost_ref[...])).reshape(nb, t, d)


def _ffn_call(x, ffn_hist, layer, w, *, nb, t):
    b, seq, d = x.shape
    grid = (b // nb, seq // t)
    tile = pl.BlockSpec((nb, t, d), lambda i, s: (i, s, 0))
    hist_specs = [] if ffn_hist is None else [
        pl.BlockSpec((None, nb, CONV_K - 1, 2 * D_FF), lambda i, s: (layer, i, 0, 0))]
    fos = pl.BlockSpec((nb, CONV_K - 1, 2 * D_FF), lambda i, s: (i, 0, 0))
    names = ('g_ffn_pre', 'w_up', 'ffn_conv_w', 'ffn_conv_b', 'w_down', 'g_ffn_post')
    return pl.pallas_call(
        functools.partial(_ffn_kernel, has_hist=ffn_hist is not None),
        grid=grid,
        in_specs=[tile] + hist_specs + [_const_spec(w[n].shape, layer) for n in names],
        out_specs=[tile, fos],
        out_shape=[jax.ShapeDtypeStruct(x.shape, _F32),
                   jax.ShapeDtypeStruct((b, CONV_K - 1, 2 * D_FF), _F32)],
        scratch_shapes=[pltpu.VMEM((2 * D_FF // LANES, nb, HIST_PAD + t, LANES), _F32),
                        pltpu.VMEM((nb * t, D_FF), _BF16)],
        compiler_params=_params(),
        name="ffn",
    )(x, *(() if ffn_hist is None else (ffn_hist,)), *[w[n] for n in names])


def _memkv_kernel(mem_ref, g_ref, wk_ref, wv_ref, k5_ref, v5_ref, kb_ref, vb_ref):
    nb, m, d = mem_ref.shape
    x = mem_ref[...].reshape(nb * m, d)
    h = _rms(x, g_ref[...]).astype(_BF16)
    for w_ref, o5_ref, ob_ref in ((wk_ref, k5_ref, kb_ref), (wv_ref, v5_ref, vb_ref)):
        kv = _dot(h, w_ref[...])
        ob_ref[...] = kv.astype(_BF16).reshape(nb, m, d)
        for n in range(nb):
            for hd in range(N_HEADS):
                o5_ref[n, :, hd, :] = kv[n * m:(n + 1) * m, hd * HEAD_DIM:(hd + 1) * HEAD_DIM]


def _memkv_call(mem, g_mem, w_k, w_v, *, nb):
    b, m, d = mem.shape
    depth = g_mem.shape[0]
    wspec = pl.BlockSpec((None, d, d), lambda l, i: (l, 0, 0))
    o5spec = pl.BlockSpec((None, nb, m, N_HEADS, HEAD_DIM), lambda l, i: (l, i, 0, 0, 0))
    obspec = pl.BlockSpec((None, nb, m, d), lambda l, i: (l, i, 0, 0))
    return pl.pallas_call(
        _memkv_kernel,
        grid=(depth, b // nb),
        in_specs=[pl.BlockSpec((nb, m, d), lambda l, i: (i, 0, 0)),
                  pl.BlockSpec((None, 1, d), lambda l, i: (l, 0, 0)), wspec, wspec],
        out_specs=[o5spec, o5spec, obspec, obspec],
        out_shape=[jax.ShapeDtypeStruct((depth, b, m, N_HEADS, HEAD_DIM), _F32)] * 2
        + [jax.ShapeDtypeStruct((depth, b, m, d), _BF16)] * 2,
        compiler_params=_params(),
        name="memkv",
    )(mem, g_mem, w_k, w_v)


def kernel(x_prompt, x_sample, mem_prompt, cache_mem_k, cache_mem_v, state_pool, state_conv, state_ffn_conv, g_mix_pre, g_mix_post, w_in, w_pool, pool_scale, conv_w, conv_b, w_out, g_attn_pre, g_attn_post, g_mem, w_q, w_k, w_v, w_o, g_ffn_pre, g_ffn_post, w_up, ffn_conv_w, ffn_conv_b, w_down):
    depth = w_in.shape[0]
    bp, seq, d = x_prompt.shape
    bs, dec_seq, _ = x_sample.shape

    def row(a):
        return a[:, None, :]

    w = {
        'g_mix_pre': row(g_mix_pre), 'g_mix_post': row(g_mix_post),
        'w_in': w_in.astype(_BF16), 'w_pool': w_pool.astype(_BF16), 'pool_scale': row(pool_scale),
        'conv_w': conv_w, 'conv_b': row(conv_b), 'w_out': w_out.astype(_BF16),
        'g_attn_pre': row(g_attn_pre), 'g_attn_post': row(g_attn_post),
        'w_q': (w_q.reshape(depth, d, d) * (HEAD_DIM ** -0.5)).astype(_BF16),
        'w_o': w_o.reshape(depth, d, d).astype(_BF16),
        'g_ffn_pre': row(g_ffn_pre), 'g_ffn_post': row(g_ffn_post),
        'w_up': w_up.astype(_BF16), 'ffn_conv_w': ffn_conv_w, 'ffn_conv_b': row(ffn_conv_b),
        'w_down': w_down.astype(_BF16),
    }

    mk_out, mv_out, mk_p, mv_p = _memkv_call(
        mem_prompt, row(g_mem), w_k.reshape(depth, d, d).astype(_BF16),
        w_v.reshape(depth, d, d).astype(_BF16), nb=2)
    mk_s, mv_s = cache_mem_k, cache_mem_v

    tp = 512
    yp, ys = x_prompt, x_sample
    pool_p, conv_p, ffn_p, pool_s, conv_s, ffn_s = [], [], [], [], [], []
    for l in range(depth):
        yp, pn, cn = _mixer_call(yp, None, l, w, nb=1, t=tp, start_pos=0)
        yp = _attn_call(yp, mk_p, mv_p, l, l, w, nb=1, t=tp)
        yp, fn = _ffn_call(yp, None, l, w, nb=1, t=tp)
        pool_p.append(pn)
        conv_p.append(cn)
        ffn_p.append(fn)
        ys, pn, cn = _mixer_call(ys, (state_pool, state_conv), l, w, nb=8, t=dec_seq, start_pos=PAST_LEN)
        ys = _attn_call(ys, mk_s, mv_s, l, l, w, nb=4, t=dec_seq)
        ys, fn = _ffn_call(ys, state_ffn_conv, l, w, nb=8, t=dec_seq)
        pool_s.append(pn)
        conv_s.append(cn)
        ffn_s.append(fn)

    return (yp, ys, mk_out, mv_out, jnp.stack(pool_p), jnp.stack(conv_p),
            jnp.stack(ffn_p), jnp.stack(pool_s), jnp.stack(conv_s), jnp.stack(ffn_s))
```

```python
import functools

import jax
import jax.numpy as jnp
from jax import lax
from jax.experimental import pallas as pl
from jax.experimental.pallas import tpu as pltpu

LANES = 128
D_MODEL = 1024
POOL_WIDTH = 512
CONV_WIDTH = 512
POOL_WINDOWS = (2, 4, 8, 16)
POOL_GROUP = 128
POOL_HIST = 15
POOL_PAD = 16
HIST_PAD = 8
CONV_K = 3
N_MEM = 256
N_HEADS = 4
HEAD_DIM = 256
D_FF = 2816
FFN_CHUNK = 256
EPS = 1e-6
PAST_LEN = 2048

VMEM_LIMIT_BYTES = 56 * 1024 * 1024

_BF16 = jnp.bfloat16
_F32 = jnp.float32


def _rms(x, g):
    ms = jnp.mean(x * x, axis=-1, keepdims=True)
    return x * lax.rsqrt(ms + EPS) * g


def _dot(a, b):
    return jnp.dot(a, b, preferred_element_type=_F32)


def _tree_sum(xs):
    while len(xs) > 1:
        xs = [xs[i] + xs[i + 1] for i in range(0, len(xs) - 1, 2)] + ([xs[-1]] if len(xs) % 2 else [])
    return xs[0]


def _conv3_tile(buf, j, t, w_ref, b_ref, cols):
    taps = [buf[j, :, pl.ds(HIST_PAD - (CONV_K - 1) + k, t), :] * w_ref[k:k + 1, cols]
            for k in range(CONV_K)]
    return taps[0] + taps[1] + (taps[2] + b_ref[:, cols])


def _const_spec(shape, layer):
    nd = len(shape)
    return pl.BlockSpec((None,) + tuple(shape[1:]), lambda b, s: (layer,) + (0,) * (nd - 1),
                        pipeline_mode=pl.Buffered(1))


def _params():
    return pltpu.CompilerParams(dimension_semantics=("arbitrary", "arbitrary"),
                                vmem_limit_bytes=VMEM_LIMIT_BYTES)


def _mixer_kernel(x_ref, *refs, start_pos, has_hist):
    ph_ref, ch_ref = refs[:2] if has_hist else (None, None)
    (gpre_ref, win_ref, wpool_ref, pscale_ref, cw_ref, cb_ref, wout_ref, gpost_ref,
     y_ref, pnew_ref, cnew_ref, pbuf, cbuf) = refs[2:] if has_hist else refs
    nb, t, d = x_ref.shape
    r = nb * t
    s = pl.program_id(1)
    n_tiles = POOL_WIDTH // LANES

    @pl.when(s == 0)
    def _():
        for j in range(n_tiles):
            cols = slice(j * LANES, (j + 1) * LANES)
            if has_hist:
                pbuf[j, :, POOL_PAD - POOL_HIST:POOL_PAD, :] = ph_ref[:, :, cols]
                cbuf[j, :, HIST_PAD - (CONV_K - 1):HIST_PAD, :] = ch_ref[:, :, cols]
            else:
                pbuf[j, :, 0:POOL_PAD, :] = jnp.zeros((nb, POOL_PAD, LANES), _F32)
                cbuf[j, :, 0:HIST_PAD, :] = jnp.zeros((nb, HIST_PAD, LANES), _F32)

    x = x_ref[...].reshape(r, d)
    h = _rms(x, gpre_ref[...]).astype(_BF16)
    proj = _dot(h, win_ref[...])
    u = proj[:, :POOL_WIDTH]
    b_gate = proj[:, POOL_WIDTH:POOL_WIDTH + CONV_WIDTH]
    cv = proj[:, POOL_WIDTH + CONV_WIDTH:POOL_WIDTH + 2 * CONV_WIDTH] * proj[:, POOL_WIDTH + 2 * CONV_WIDTH:]
    for j in range(n_tiles):
        cols = slice(j * LANES, (j + 1) * LANES)
        pbuf[j, :, POOL_PAD:, :] = u[:, cols].reshape(nb, t, LANES)
        cbuf[j, :, HIST_PAD:, :] = cv[:, cols].reshape(nb, t, LANES)

    pos = start_pos + s * t + lax.broadcasted_iota(jnp.int32, (nb, t, 1), 1)
    ys = []
    for g, w in enumerate(POOL_WINDOWS):
        cols = slice(g * POOL_GROUP, (g + 1) * POOL_GROUP)
        if w <= 8:
            win = _tree_sum([pbuf[g, :, pl.ds(POOL_PAD - i, t), :] for i in range(w)])
        else:
            s8 = _tree_sum([pbuf[g, :, pl.ds(8 - i, t + 8), :] for i in range(8)])
            win = s8[:, 8:, :] + s8[:, :t, :]
        inv_cnt = 1.0 / jnp.minimum(w, pos + 1).astype(_F32)
        dlt = ((win * inv_cnt).reshape(r, POOL_GROUP) - u[:, cols]).astype(_BF16)
        ys.append(_dot(dlt, wpool_ref[g]) * pscale_ref[:, cols])

    for j in range(n_tiles):
        cols = slice(j * LANES, (j + 1) * LANES)
        zc = _conv3_tile(cbuf, j, t, cw_ref, cb_ref, cols).reshape(r, LANES)
        ys.append(b_gate[:, cols] * zc)

    for j in range(n_tiles):
        cols = slice(j * LANES, (j + 1) * LANES)
        pnew_ref[:, :, cols] = pbuf[j, :, pl.ds(POOL_PAD + t - POOL_HIST, POOL_HIST), :]
        cnew_ref[:, :, cols] = cbuf[j, :, pl.ds(HIST_PAD + t - (CONV_K - 1), CONV_K - 1), :]
        pbuf[j, :, 0:POOL_PAD, :] = pbuf[j, :, t:, :]
        cbuf[j, :, 0:HIST_PAD, :] = cbuf[j, :, t:, :]

    ycat = jnp.concatenate(ys, axis=-1).astype(_BF16)
    y = _dot(ycat, wout_ref[...])
    y_ref[...] = (x + _rms(y, gpost_ref[...])).reshape(nb, t, d)


def _mixer_call(x, hists, layer, w, *, nb, t, start_pos):
    b, seq, d = x.shape
    grid = (b // nb, seq // t)
    tile = pl.BlockSpec((nb, t, d), lambda i, s: (i, s, 0))
    hist_specs = [] if hists is None else [
        pl.BlockSpec((None, nb, POOL_HIST, POOL_WIDTH), lambda i, s: (layer, i, 0, 0)),
        pl.BlockSpec((None, nb, CONV_K - 1, CONV_WIDTH), lambda i, s: (layer, i, 0, 0))]
    pos_ = pl.BlockSpec((nb, POOL_HIST, POOL_WIDTH), lambda i, s: (i, 0, 0))
    cos_ = pl.BlockSpec((nb, CONV_K - 1, CONV_WIDTH), lambda i, s: (i, 0, 0))
    names = ('g_mix_pre', 'w_in', 'w_pool', 'pool_scale', 'conv_w', 'conv_b', 'w_out', 'g_mix_post')
    return pl.pallas_call(
        functools.partial(_mixer_kernel, start_pos=start_pos, has_hist=hists is not None),
        grid=grid,
        in_specs=[tile] + hist_specs + [_const_spec(w[n].shape, layer) for n in names],
        out_specs=[tile, pos_, cos_],
        out_shape=[jax.ShapeDtypeStruct(x.shape, _F32),
                   jax.ShapeDtypeStruct((b, POOL_HIST, POOL_WIDTH), _F32),
                   jax.ShapeDtypeStruct((b, CONV_K - 1, CONV_WIDTH), _F32)],
        scratch_shapes=[pltpu.VMEM((POOL_WIDTH // LANES, nb, POOL_PAD + t, LANES), _F32),
                        pltpu.VMEM((CONV_WIDTH // LANES, nb, HIST_PAD + t, LANES), _F32)],
        compiler_params=_params(),
        name="mixer",
    )(x, *(hists or ()), *[w[n] for n in names])


def _kv_copies(k_hbm, v_hbm, kbuf, vbuf, sem, kv_layer, step, slot, nb):
    cps = []
    for n in range(nb):
        for hd in range(N_HEADS):
            cps.append(pltpu.make_async_copy(k_hbm.at[kv_layer, step * nb + n, :, hd, :],
                                             kbuf.at[slot, n, hd], sem.at[slot, 0]))
            cps.append(pltpu.make_async_copy(v_hbm.at[kv_layer, step * nb + n, :, hd, :],
                                             vbuf.at[slot, n, hd], sem.at[slot, 1]))
    return cps


def _attn_kernel(x_ref, k_ref, v_ref, gpre_ref, wq_ref, wo_ref, gpost_ref, y_ref, *kv_scratch, kv_layer):
    nb, t, d = x_ref.shape
    r = nb * t

    if kv_scratch:
        kbuf, vbuf, sem = kv_scratch
        i = pl.program_id(0)
        slot = i % 2

        @pl.when(i == 0)
        def _():
            for c in _kv_copies(k_ref, v_ref, kbuf, vbuf, sem, kv_layer, 0, 0, nb):
                c.start()

        @pl.when(i + 1 < pl.num_programs(0))
        def _():
            for c in _kv_copies(k_ref, v_ref, kbuf, vbuf, sem, kv_layer, i + 1, 1 - slot, nb):
                c.start()

        for c in _kv_copies(k_ref, v_ref, kbuf, vbuf, sem, kv_layer, i, slot, nb):
            c.wait()

        def k_head(n, hd):
            return kbuf[slot, n, hd].astype(_BF16)

        def v_head(n, hd):
            return vbuf[slot, n, hd].astype(_BF16)
    else:
        def k_head(n, hd):
            return k_ref[n, :, hd * HEAD_DIM:(hd + 1) * HEAD_DIM]

        def v_head(n, hd):
            return v_ref[n, :, hd * HEAD_DIM:(hd + 1) * HEAD_DIM]

    x = x_ref[...].reshape(r, d)
    h = _rms(x, gpre_ref[...]).astype(_BF16)
    q = _dot(h, wq_ref[...]).astype(_BF16)
    scores = {}
    for n in range(nb):
        for hd in range(N_HEADS):
            qh = q[n * t:(n + 1) * t, hd * HEAD_DIM:(hd + 1) * HEAD_DIM]
            scores[n, hd] = lax.dot_general(qh, k_head(n, hd), (((1,), (1,)), ((), ())),
                                            preferred_element_type=_F32)
    rows = []
    for n in range(nb):
        heads = []
        for hd in range(N_HEADS):
            sc = scores[n, hd]
            e = jnp.exp(sc - jnp.max(sc, axis=-1, keepdims=True))
            p = (e / jnp.sum(e, axis=-1, keepdims=True)).astype(_BF16)
            heads.append(_dot(p, v_head(n, hd)))
        rows.append(jnp.concatenate(heads, axis=-1))
    o = (rows[0] if nb == 1 else jnp.concatenate(rows, axis=0)).astype(_BF16)
    y = _dot(o, wo_ref[...])
    y_ref[...] = (x + _rms(y, gpost_ref[...])).reshape(nb, t, d)


def _attn_call(x, mem_k, mem_v, kv_layer, layer, w, *, nb, t):
    b, seq, d = x.shape
    grid = (b // nb, seq // t)
    tile = pl.BlockSpec((nb, t, d), lambda i, s: (i, s, 0))
    if mem_k.ndim == 4:
        kvs = pl.BlockSpec((None, nb, N_MEM, d), lambda i, s: (kv_layer, i, 0, 0))
        scratch = []
    else:
        assert seq == t, "the cache path prefetches per stream block along grid axis 0 only"
        kvs = pl.BlockSpec(memory_space=pl.ANY)
        scratch = [pltpu.VMEM((2, nb, N_HEADS, N_MEM, HEAD_DIM), _F32),
                   pltpu.VMEM((2, nb, N_HEADS, N_MEM, HEAD_DIM), _F32),
                   pltpu.SemaphoreType.DMA((2, 2))]
    names = ('g_attn_pre', 'w_q', 'w_o', 'g_attn_post')
    return pl.pallas_call(
        functools.partial(_attn_kernel, kv_layer=kv_layer),
        grid=grid,
        in_specs=[tile, kvs, kvs] + [_const_spec(w[n].shape, layer) for n in names],
        out_specs=tile,
        out_shape=jax.ShapeDtypeStruct(x.shape, _F32),
        scratch_shapes=scratch,
        compiler_params=_params(),
        name="attn",
    )(x, mem_k, mem_v, *[w[n] for n in names])


def _ffn_kernel(x_ref, *refs, has_hist):
    fh_ref = refs[0] if has_hist else None
    (gpre_ref, wup_ref, cw_ref, cb_ref, wdown_ref, gpost_ref,
     y_ref, fnew_ref, ubuf, hid) = refs[1:] if has_hist else refs
    nb, t, d = x_ref.shape
    r = nb * t
    n_tiles = 2 * D_FF // LANES
    per_chunk = FFN_CHUNK // LANES

    @pl.when(pl.program_id(1) == 0)
    def _():
        for j in range(n_tiles):
            if has_hist:
                ubuf[j, :, HIST_PAD - (CONV_K - 1):HIST_PAD, :] = fh_ref[:, :, j * LANES:(j + 1) * LANES]
            else:
                ubuf[j, :, 0:HIST_PAD, :] = jnp.zeros((nb, HIST_PAD, LANES), _F32)

    x = x_ref[...].reshape(r, d)
    h = _rms(x, gpre_ref[...]).astype(_BF16)

    def conv_cols(lo):
        up = _dot(h, wup_ref[:, lo:lo + FFN_CHUNK])
        outs = []
        for k in range(per_chunk):
            j = lo // LANES + k
            ubuf[j, :, HIST_PAD:, :] = up[:, k * LANES:(k + 1) * LANES].reshape(nb, t, LANES)
            cols = slice(j * LANES, (j + 1) * LANES)
            outs.append(_conv3_tile(ubuf, j, t, cw_ref, cb_ref, cols).reshape(r, LANES))
        return jnp.concatenate(outs, axis=-1)

    for c in range(D_FF // FFN_CHUNK):
        gate = conv_cols(c * FFN_CHUNK)
        value = conv_cols(D_FF + c * FFN_CHUNK)
        hid[:, c * FFN_CHUNK:(c + 1) * FFN_CHUNK] = (gate * jax.nn.sigmoid(gate) * value).astype(_BF16)

    for j in range(n_tiles):
        fnew_ref[:, :, j * LANES:(j + 1) * LANES] = ubuf[j, :, pl.ds(HIST_PAD + t - (CONV_K - 1), CONV_K - 1), :]
        ubuf[j, :, 0:HIST_PAD, :] = ubuf[j, :, t:, :]

    y = _dot(hid[...], wdown_ref[...])
    y_ref[...] = (x + _rms(y, gpost_ref[...])).reshape(nb, t, d)


def _ffn_call(x, ffn_hist, layer, w, *, nb, t):
    b, seq, d = x.shape
    grid = (b // nb, seq // t)
    tile = pl.BlockSpec((nb, t, d), lambda i, s: (i, s, 0))
    hist_specs = [] if ffn_hist is None else [
        pl.BlockSpec((None, nb, CONV_K - 1, 2 * D_FF), lambda i, s: (layer, i, 0, 0))]
    fos = pl.BlockSpec((nb, CONV_K - 1, 2 * D_FF), lambda i, s: (i, 0, 0))
    names = ('g_ffn_pre', 'w_up', 'ffn_conv_w', 'ffn_conv_b', 'w_down', 'g_ffn_post')
    return pl.pallas_call(
        functools.partial(_ffn_kernel, has_hist=ffn_hist is not None),
        grid=grid,
        in_specs=[tile] + hist_specs + [_const_spec(w[n].shape, layer) for n in names],
        out_specs=[tile, fos],
        out_shape=[jax.ShapeDtypeStruct(x.shape, _F32),
                   jax.ShapeDtypeStruct((b, CONV_K - 1, 2 * D_FF), _F32)],
        scratch_shapes=[pltpu.VMEM((2 * D_FF // LANES, nb, HIST_PAD + t, LANES), _F32),
                        pltpu.VMEM((nb * t, D_FF), _BF16)],
        compiler_params=_params(),
        name="ffn",
    )(x, *(() if ffn_hist is None else (ffn_hist,)), *[w[n] for n in names])


def _memkv_kernel(mem_ref, g_ref, wk_ref, wv_ref, k5_ref, v5_ref, kb_ref, vb_ref):
    nb, m, d = mem_ref.shape
    x = mem_ref[...].reshape(nb * m, d)
    h = _rms(x, g_ref[...]).astype(_BF16)
    for w_ref, o5_ref, ob_ref in ((wk_ref, k5_ref, kb_ref), (wv_ref, v5_ref, vb_ref)):
        kv = _dot(h, w_ref[...])
        ob_ref[...] = kv.astype(_BF16).reshape(nb, m, d)
        for n in range(nb):
            for hd in range(N_HEADS):
                o5_ref[n, :, hd, :] = kv[n * m:(n + 1) * m, hd * HEAD_DIM:(hd + 1) * HEAD_DIM]


def _memkv_call(mem, g_mem, w_k, w_v, *, nb):
    b, m, d = mem.shape
    depth = g_mem.shape[0]
    wspec = pl.BlockSpec((None, d, d), lambda l, i: (l, 0, 0))
    o5spec = pl.BlockSpec((None, nb, m, N_HEADS, HEAD_DIM), lambda l, i: (l, i, 0, 0, 0))
    obspec = pl.BlockSpec((None, nb, m, d), lambda l, i: (l, i, 0, 0))
    return pl.pallas_call(
        _memkv_kernel,
        grid=(depth, b // nb),
        in_specs=[pl.BlockSpec((nb, m, d), lambda l, i: (i, 0, 0)),
                  pl.BlockSpec((None, 1, d), lambda l, i: (l, 0, 0)), wspec, wspec],
        out_specs=[o5spec, o5spec, obspec, obspec],
        out_shape=[jax.ShapeDtypeStruct((depth, b, m, N_HEADS, HEAD_DIM), _F32)] * 2
        + [jax.ShapeDtypeStruct((depth, b, m, d), _BF16)] * 2,
        compiler_params=_params(),
        name="memkv",
    )(mem, g_mem, w_k, w_v)


def kernel(x_prompt, x_sample, mem_prompt, cache_mem_k, cache_mem_v, state_pool, state_conv, state_ffn_conv, g_mix_pre, g_mix_post, w_in, w_pool, pool_scale, conv_w, conv_b, w_out, g_attn_pre, g_attn_post, g_mem, w_q, w_k, w_v, w_o, g_ffn_pre, g_ffn_post, w_up, ffn_conv_w, ffn_conv_b, w_down):
    depth = w_in.shape[0]
    bp, seq, d = x_prompt.shape
    bs, dec_seq, _ = x_sample.shape

    def row(a):
        return a[:, None, :]

    w = {
        'g_mix_pre': row(g_mix_pre), 'g_mix_post': row(g_mix_post),
        'w_in': w_in.astype(_BF16), 'w_pool': w_pool.astype(_BF16), 'pool_scale': row(pool_scale),
        'conv_w': conv_w, 'conv_b': row(conv_b), 'w_out': w_out.astype(_BF16),
        'g_attn_pre': row(g_attn_pre), 'g_attn_post': row(g_attn_post),
        'w_q': (w_q * (HEAD_DIM ** -0.5)).astype(_BF16).reshape(depth, d, d),
        'w_o': w_o.reshape(depth, d, d).astype(_BF16),
        'g_ffn_pre': row(g_ffn_pre), 'g_ffn_post': row(g_ffn_post),
        'w_up': w_up.astype(_BF16), 'ffn_conv_w': ffn_conv_w, 'ffn_conv_b': row(ffn_conv_b),
        'w_down': w_down.astype(_BF16),
    }

    mk_out, mv_out, mk_p, mv_p = _memkv_call(
        mem_prompt, row(g_mem), w_k.astype(_BF16).reshape(depth, d, d),
        w_v.astype(_BF16).reshape(depth, d, d), nb=2)
    mk_s, mv_s = cache_mem_k, cache_mem_v

    tp = 512
    yp, ys = x_prompt, x_sample
    pool_p, conv_p, ffn_p, pool_s, conv_s, ffn_s = [], [], [], [], [], []
    for l in range(depth):
        yp, pn, cn = _mixer_call(yp, None, l, w, nb=1, t=tp, start_pos=0)
        yp = _attn_call(yp, mk_p, mv_p, l, l, w, nb=1, t=tp)
        yp, fn = _ffn_call(yp, None, l, w, nb=1, t=tp)
        pool_p.append(pn)
        conv_p.append(cn)
        ffn_p.append(fn)
        ys, pn, cn = _mixer_call(ys, (state_pool, state_conv), l, w, nb=8, t=dec_seq, start_pos=PAST_LEN)
        ys = _attn_call(ys, mk_s, mv_s, l, l, w, nb=4, t=dec_seq)
        ys, fn = _ffn_call(ys, state_ffn_conv, l, w, nb=8, t=dec_seq)
        pool_s.append(pn)
        conv_s.append(cn)
        ffn_s.append(fn)

    return (yp, ys, mk_out, mv_out, jnp.stack(pool_p), jnp.stack(conv_p),
            jnp.stack(ffn_p), jnp.stack(pool_s), jnp.stack(conv_s), jnp.stack(ffn_s))
```

```python
import functools

import jax
import jax.numpy as jnp
from jax import lax
from jax.experimental import pallas as pl
from jax.experimental.pallas import tpu as pltpu

LANES = 128
D_MODEL = 1024
POOL_WIDTH = 512
CONV_WIDTH = 512
POOL_WINDOWS = (2, 4, 8, 16)
POOL_GROUP = 128
POOL_HIST = 15
POOL_PAD = 16
HIST_PAD = 8
CONV_K = 3
N_MEM = 256
N_HEADS = 4
HEAD_DIM = 256
D_FF = 2816
FFN_CHUNK = 256
EPS = 1e-6
PAST_LEN = 2048

VMEM_LIMIT_BYTES = 56 * 1024 * 1024

_BF16 = jnp.bfloat16
_F32 = jnp.float32


def _rms(x, g):
    ms = jnp.mean(x * x, axis=-1, keepdims=True)
    return x * lax.rsqrt(ms + EPS) * g


def _dot(a, b):
    return jnp.dot(a, b, preferred_element_type=_F32)


def _tree_sum(xs):
    while len(xs) > 1:
        xs = [xs[i] + xs[i + 1] for i in range(0, len(xs) - 1, 2)] + ([xs[-1]] if len(xs) % 2 else [])
    return xs[0]


def _conv3_tile(buf, j, t, w_ref, b_ref, cols):
    taps = [buf[j, :, pl.ds(HIST_PAD - (CONV_K - 1) + k, t), :] * w_ref[k:k + 1, cols]
            for k in range(CONV_K)]
    return taps[0] + taps[1] + (taps[2] + b_ref[:, cols])


def _const_spec(shape, layer):
    nd = len(shape)
    return pl.BlockSpec((None,) + tuple(shape[1:]), lambda b, s: (layer,) + (0,) * (nd - 1),
                        pipeline_mode=pl.Buffered(1))


def _params():
    return pltpu.CompilerParams(dimension_semantics=("arbitrary", "arbitrary"),
                                vmem_limit_bytes=VMEM_LIMIT_BYTES)


def _mixer_kernel(x_ref, *refs, start_pos, has_hist):
    ph_ref, ch_ref = refs[:2] if has_hist else (None, None)
    (gpre_ref, win_ref, wpool_ref, pscale_ref, cw_ref, cb_ref, wout_ref, gpost_ref,
     y_ref, pnew_ref, cnew_ref, pbuf, cbuf) = refs[2:] if has_hist else refs
    nb, t, d = x_ref.shape
    r = nb * t
    s = pl.program_id(1)
    n_tiles = POOL_WIDTH // LANES

    @pl.when(s == 0)
    def _():
        for j in range(n_tiles):
            cols = slice(j * LANES, (j + 1) * LANES)
            if has_hist:
                pbuf[j, :, POOL_PAD - POOL_HIST:POOL_PAD, :] = ph_ref[:, :, cols]
                cbuf[j, :, HIST_PAD - (CONV_K - 1):HIST_PAD, :] = ch_ref[:, :, cols]
            else:
                pbuf[j, :, 0:POOL_PAD, :] = jnp.zeros((nb, POOL_PAD, LANES), _F32)
                cbuf[j, :, 0:HIST_PAD, :] = jnp.zeros((nb, HIST_PAD, LANES), _F32)

    x = x_ref[...].reshape(r, d)
    h = _rms(x, gpre_ref[...]).astype(_BF16)
    proj = _dot(h, win_ref[...])
    u = proj[:, :POOL_WIDTH]
    b_gate = proj[:, POOL_WIDTH:POOL_WIDTH + CONV_WIDTH]
    cv = proj[:, POOL_WIDTH + CONV_WIDTH:POOL_WIDTH + 2 * CONV_WIDTH] * proj[:, POOL_WIDTH + 2 * CONV_WIDTH:]
    for j in range(n_tiles):
        cols = slice(j * LANES, (j + 1) * LANES)
        pbuf[j, :, POOL_PAD:, :] = u[:, cols].reshape(nb, t, LANES)
        cbuf[j, :, HIST_PAD:, :] = cv[:, cols].reshape(nb, t, LANES)

    pos = start_pos + s * t + lax.broadcasted_iota(jnp.int32, (nb, t, 1), 1)
    ys = []
    for g, w in enumerate(POOL_WINDOWS):
        cols = slice(g * POOL_GROUP, (g + 1) * POOL_GROUP)
        if w <= 8:
            win = _tree_sum([pbuf[g, :, pl.ds(POOL_PAD - i, t), :] for i in range(w)])
        else:
            s8 = _tree_sum([pbuf[g, :, pl.ds(8 - i, t + 8), :] for i in range(8)])
            win = s8[:, 8:, :] + s8[:, :t, :]
        inv_cnt = 1.0 / jnp.minimum(w, pos + 1).astype(_F32)
        dlt = ((win * inv_cnt).reshape(r, POOL_GROUP) - u[:, cols]).astype(_BF16)
        ys.append(_dot(dlt, wpool_ref[g]) * pscale_ref[:, cols])

    for j in range(n_tiles):
        cols = slice(j * LANES, (j + 1) * LANES)
        zc = _conv3_tile(cbuf, j, t, cw_ref, cb_ref, cols).reshape(r, LANES)
        ys.append(b_gate[:, cols] * zc)

    for j in range(n_tiles):
        cols = slice(j * LANES, (j + 1) * LANES)
        pnew_ref[:, :, cols] = pbuf[j, :, pl.ds(POOL_PAD + t - POOL_HIST, POOL_HIST), :]
        cnew_ref[:, :, cols] = cbuf[j, :, pl.ds(HIST_PAD + t - (CONV_K - 1), CONV_K - 1), :]
        pbuf[j, :, 0:POOL_PAD, :] = pbuf[j, :, t:, :]
        cbuf[j, :, 0:HIST_PAD, :] = cbuf[j, :, t:, :]

    ycat = jnp.concatenate(ys, axis=-1).astype(_BF16)
    y = _dot(ycat, wout_ref[...])
    y_ref[...] = (x + _rms(y, gpost_ref[...])).reshape(nb, t, d)


def _mixer_call(x, hists, layer, w, *, nb, t, start_pos):
    b, seq, d = x.shape
    grid = (b // nb, seq // t)
    tile = pl.BlockSpec((nb, t, d), lambda i, s: (i, s, 0))
    hist_specs = [] if hists is None else [
        pl.BlockSpec((None, nb, POOL_HIST, POOL_WIDTH), lambda i, s: (layer, i, 0, 0)),
        pl.BlockSpec((None, nb, CONV_K - 1, CONV_WIDTH), lambda i, s: (layer, i, 0, 0))]
    pos_ = pl.BlockSpec((nb, POOL_HIST, POOL_WIDTH), lambda i, s: (i, 0, 0))
    cos_ = pl.BlockSpec((nb, CONV_K - 1, CONV_WIDTH), lambda i, s: (i, 0, 0))
    names = ('g_mix_pre', 'w_in', 'w_pool', 'pool_scale', 'conv_w', 'conv_b', 'w_out', 'g_mix_post')
    return pl.pallas_call(
        functools.partial(_mixer_kernel, start_pos=start_pos, has_hist=hists is not None),
        grid=grid,
        in_specs=[tile] + hist_specs + [_const_spec(w[n].shape, layer) for n in names],
        out_specs=[tile, pos_, cos_],
        out_shape=[jax.ShapeDtypeStruct(x.shape, _F32),
                   jax.ShapeDtypeStruct((b, POOL_HIST, POOL_WIDTH), _F32),
                   jax.ShapeDtypeStruct((b, CONV_K - 1, CONV_WIDTH), _F32)],
        scratch_shapes=[pltpu.VMEM((POOL_WIDTH // LANES, nb, POOL_PAD + t, LANES), _F32),
                        pltpu.VMEM((CONV_WIDTH // LANES, nb, HIST_PAD + t, LANES), _F32)],
        compiler_params=_params(),
        name="mixer",
    )(x, *(hists or ()), *[w[n] for n in names])


def _kv_copies(k_hbm, v_hbm, kbuf, vbuf, sem, kv_layer, step, slot, nb):
    cps = []
    for n in range(nb):
        for hd in range(N_HEADS):
            cps.append(pltpu.make_async_copy(k_hbm.at[kv_layer, step * nb + n, :, hd, :],
                                             kbuf.at[slot, n, hd], sem.at[slot, 0]))
            cps.append(pltpu.make_async_copy(v_hbm.at[kv_layer, step * nb + n, :, hd, :],
                                             vbuf.at[slot, n, hd], sem.at[slot, 1]))
    return cps


def _attn_kernel(x_ref, k_ref, v_ref, gpre_ref, wq_ref, wo_ref, gpost_ref, y_ref, *kv_scratch, kv_layer):
    nb, t, d = x_ref.shape
    r = nb * t

    if kv_scratch:
        kbuf, vbuf, sem = kv_scratch
        i = pl.program_id(0)
        slot = i % 2

        @pl.when(i == 0)
        def _():
            for c in _kv_copies(k_ref, v_ref, kbuf, vbuf, sem, kv_layer, 0, 0, nb):
                c.start()

        @pl.when(i + 1 < pl.num_programs(0))
        def _():
            for c in _kv_copies(k_ref, v_ref, kbuf, vbuf, sem, kv_layer, i + 1, 1 - slot, nb):
                c.start()

        for c in _kv_copies(k_ref, v_ref, kbuf, vbuf, sem, kv_layer, i, slot, nb):
            c.wait()

        def k_head(n, hd):
            return kbuf[slot, n, hd].astype(_BF16)

        def v_head(n, hd):
            return vbuf[slot, n, hd].astype(_BF16)
    else:
        def k_head(n, hd):
            return k_ref[n, :, hd * HEAD_DIM:(hd + 1) * HEAD_DIM]

        def v_head(n, hd):
            return v_ref[n, :, hd * HEAD_DIM:(hd + 1) * HEAD_DIM]

    x = x_ref[...].reshape(r, d)
    h = _rms(x, gpre_ref[...]).astype(_BF16)
    q = _dot(h, wq_ref[...]).astype(_BF16)
    scores = {}
    for n in range(nb):
        for hd in range(N_HEADS):
            qh = q[n * t:(n + 1) * t, hd * HEAD_DIM:(hd + 1) * HEAD_DIM]
            scores[n, hd] = lax.dot_general(qh, k_head(n, hd), (((1,), (1,)), ((), ())),
                                            preferred_element_type=_F32)
    rows = []
    for n in range(nb):
        heads = []
        for hd in range(N_HEADS):
            sc = scores[n, hd]
            e = jnp.exp(sc - jnp.max(sc, axis=-1, keepdims=True))
            p = (e / jnp.sum(e, axis=-1, keepdims=True)).astype(_BF16)
            heads.append(_dot(p, v_head(n, hd)))
        rows.append(jnp.concatenate(heads, axis=-1))
    o = (rows[0] if nb == 1 else jnp.concatenate(rows, axis=0)).astype(_BF16)
    y = _dot(o, wo_ref[...])
    y_ref[...] = (x + _rms(y, gpost_ref[...])).reshape(nb, t, d)


def _attn_call(x, mem_k, mem_v, kv_layer, layer, w, *, nb, t):
    b, seq, d = x.shape
    grid = (b // nb, seq // t)
    tile = pl.BlockSpec((nb, t, d), lambda i, s: (i, s, 0))
    if mem_k.ndim == 4:
        kvs = pl.BlockSpec((None, nb, N_MEM, d), lambda i, s: (kv_layer, i, 0, 0))
        scratch = []
    else:
        assert seq == t, "the cache path prefetches per stream block along grid axis 0 only"
        kvs = pl.BlockSpec(memory_space=pl.ANY)
        scratch = [pltpu.VMEM((2, nb, N_HEADS, N_MEM, HEAD_DIM), _F32),
                   pltpu.VMEM((2, nb, N_HEADS, N_MEM, HEAD_DIM), _F32),
                   pltpu.SemaphoreType.DMA((2, 2))]
    names = ('g_attn_pre', 'w_q', 'w_o', 'g_attn_post')
    return pl.pallas_call(
        functools.partial(_attn_kernel, kv_layer=kv_layer),
        grid=grid,
        in_specs=[tile, kvs, kvs] + [_const_spec(w[n].shape, layer) for n in names],
        out_specs=tile,
        out_shape=jax.ShapeDtypeStruct(x.shape, _F32),
        scratch_shapes=scratch,
        compiler_params=_params(),
        name="attn",
    )(x, mem_k, mem_v, *[w[n] for n in names])


def _ffn_kernel(x_ref, *refs, has_hist):
    fh_ref = refs[0] if has_hist else None
    (gpre_ref, wup_ref, cw_ref, cb_ref, wdown_ref, gpost_ref,
     y_ref, fnew_ref, ubuf, hid) = refs[1:] if has_hist else refs
    nb, t, d = x_ref.shape
    r = nb * t
    n_tiles = 2 * D_FF // LANES
    per_chunk = FFN_CHUNK // LANES

    @pl.when(pl.program_id(1) == 0)
    def _():
        for j in range(n_tiles):
            if has_hist:
                ubuf[j, :, HIST_PAD - (CONV_K - 1):HIST_PAD, :] = fh_ref[:, :, j * LANES:(j + 1) * LANES]
            else:
                ubuf[j, :, 0:HIST_PAD, :] = jnp.zeros((nb, HIST_PAD, LANES), _F32)

    x = x_ref[...].reshape(r, d)
    h = _rms(x, gpre_ref[...]).astype(_BF16)

    def conv_cols(lo):
        up = _dot(h, wup_ref[:, lo:lo + FFN_CHUNK])
        outs = []
        for k in range(per_chunk):
            j = lo // LANES + k
            ubuf[j, :, HIST_PAD:, :] = up[:, k * LANES:(k + 1) * LANES].reshape(nb, t, LANES)
            cols = slice(j * LANES, (j + 1) * LANES)
            outs.append(_conv3_tile(ubuf, j, t, cw_ref, cb_ref, cols).reshape(r, LANES))
        return jnp.concatenate(outs, axis=-1)

    for c in range(D_FF // FFN_CHUNK):
        gate = conv_cols(c * FFN_CHUNK)
        value = conv_cols(D_FF + c * FFN_CHUNK)
        hid[:, c * FFN_CHUNK:(c + 1) * FFN_CHUNK] = (gate * jax.nn.sigmoid(gate) * value).astype(_BF16)

    for j in range(n_tiles):
        fnew_ref[:, :, j * LANES:(j + 1) * LANES] = ubuf[j, :, pl.ds(HIST_PAD + t - (CONV_K - 1), CONV_K - 1), :]
        ubuf[j, :, 0:HIST_PAD, :] = ubuf[j, :, t:, :]

    y = _dot(hid[...], wdown_ref[...])
    y_ref[...] = (x + _rms(y, gpost_ref[...])).reshape(nb, t, d)


def _ffn_call(x, ffn_hist, layer, w, *, nb, t):
    b, seq, d = x.shape
    grid = (b // nb, seq // t)
    tile = pl.BlockSpec((nb, t, d), lambda i, s: (i, s, 0))
    hist_specs = [] if ffn_hist is None else [
        pl.BlockSpec((None, nb, CONV_K - 1, 2 * D_FF), lambda i, s: (layer, i, 0, 0))]
    fos = pl.BlockSpec((nb, CONV_K - 1, 2 * D_FF), lambda i, s: (i, 0, 0))
    names = ('g_ffn_pre', 'w_up', 'ffn_conv_w', 'ffn_conv_b', 'w_down', 'g_ffn_post')
    return pl.pallas_call(
        functools.partial(_ffn_kernel, has_hist=ffn_hist is not None),
        grid=grid,
        in_specs=[tile] + hist_specs + [_const_spec(w[n].shape, layer) for n in names],
        out_specs=[tile, fos],
        out_shape=[jax.ShapeDtypeStruct(x.shape, _F32),
                   jax.ShapeDtypeStruct((b, CONV_K - 1, 2 * D_FF), _F32)],
        scratch_shapes=[pltpu.VMEM((2 * D_FF // LANES, nb, HIST_PAD + t, LANES), _F32),
                        pltpu.VMEM((nb * t, D_FF), _BF16)],
        compiler_params=_params(),
        name="ffn",
    )(x, *(() if ffn_hist is None else (ffn_hist,)), *[w[n] for n in names])


def _memkv_kernel(mem_ref, g_ref, wk_ref, wv_ref, k5_ref, v5_ref, kb_ref, vb_ref):
    nb, m, d = mem_ref.shape
    x = mem_ref[...].reshape(nb * m, d)
    h = _rms(x, g_ref[...]).astype(_BF16)
    for w_ref, o5_ref, ob_ref in ((wk_ref, k5_ref, kb_ref), (wv_ref, v5_ref, vb_ref)):
        kv = _dot(h, w_ref[...])
        ob_ref[...] = kv.astype(_BF16).reshape(nb, m, d)
        for n in range(nb):
            for hd in range(N_HEADS):
                o5_ref[n, :, hd, :] = kv[n * m:(n + 1) * m, hd * HEAD_DIM:(hd + 1) * HEAD_DIM]


def _memkv_call(mem, g_mem, w_k, w_v, *, nb):
    b, m, d = mem.shape
    depth = g_mem.shape[0]
    wspec = pl.BlockSpec((None, d, d), lambda l, i: (l, 0, 0))
    o5spec = pl.BlockSpec((None, nb, m, N_HEADS, HEAD_DIM), lambda l, i: (l, i, 0, 0, 0))
    obspec = pl.BlockSpec((None, nb, m, d), lambda l, i: (l, i, 0, 0))
    return pl.pallas_call(
        _memkv_kernel,
        grid=(depth, b // nb),
        in_specs=[pl.BlockSpec((nb, m, d), lambda l, i: (i, 0, 0)),
                  pl.BlockSpec((None, 1, d), lambda l, i: (l, 0, 0)), wspec, wspec],
        out_specs=[o5spec, o5spec, obspec, obspec],
        out_shape=[jax.ShapeDtypeStruct((depth, b, m, N_HEADS, HEAD_DIM), _F32)] * 2
        + [jax.ShapeDtypeStruct((depth, b, m, d), _BF16)] * 2,
        compiler_params=_params(),
        name="memkv",
    )(mem, g_mem, w_k, w_v)


def kernel(x_prompt, x_sample, mem_prompt, cache_mem_k, cache_mem_v, state_pool, state_conv, state_ffn_conv, g_mix_pre, g_mix_post, w_in, w_pool, pool_scale, conv_w, conv_b, w_out, g_attn_pre, g_attn_post, g_mem, w_q, w_k, w_v, w_o, g_ffn_pre, g_ffn_post, w_up, ffn_conv_w, ffn_conv_b, w_down):
    depth = w_in.shape[0]
    bp, seq, d = x_prompt.shape
    bs, dec_seq, _ = x_sample.shape

    def row(a):
        return a[:, None, :]

    w = {
        'g_mix_pre': row(g_mix_pre), 'g_mix_post': row(g_mix_post),
        'w_in': w_in.astype(_BF16), 'w_pool': w_pool.astype(_BF16), 'pool_scale': row(pool_scale),
        'conv_w': conv_w, 'conv_b': row(conv_b), 'w_out': w_out.astype(_BF16),
        'g_attn_pre': row(g_attn_pre), 'g_attn_post': row(g_attn_post),
        'w_q': (w_q.reshape(depth, d, d) * (HEAD_DIM ** -0.5)).astype(_BF16),
        'w_o': w_o.reshape(depth, d, d).astype(_BF16),
        'g_ffn_pre': row(g_ffn_pre), 'g_ffn_post': row(g_ffn_post),
        'w_up': w_up.astype(_BF16), 'ffn_conv_w': ffn_conv_w, 'ffn_conv_b': row(ffn_conv_b),
        'w_down': w_down.astype(_BF16),
    }

    mk_out, mv_out, mk_p, mv_p = _memkv_call(
        mem_prompt, row(g_mem), w_k.reshape(depth, d, d).astype(_BF16),
        w_v.reshape(depth, d, d).astype(_BF16), nb=2)
    mk_s, mv_s = cache_mem_k, cache_mem_v

    tp = 512
    tp_wide = 1024
    yp, ys = x_prompt, x_sample
    pool_p, conv_p, ffn_p, pool_s, conv_s, ffn_s = [], [], [], [], [], []
    for l in range(depth):
        yp, pn, cn = _mixer_call(yp, None, l, w, nb=1, t=tp_wide, start_pos=0)
        yp = _attn_call(yp, mk_p, mv_p, l, l, w, nb=1, t=tp_wide)
        yp, fn = _ffn_call(yp, None, l, w, nb=1, t=tp)
        pool_p.append(pn)
        conv_p.append(cn)
        ffn_p.append(fn)
        ys, pn, cn = _mixer_call(ys, (state_pool, state_conv), l, w, nb=8, t=dec_seq, start_pos=PAST_LEN)
        ys = _attn_call(ys, mk_s, mv_s, l, l, w, nb=4, t=dec_seq)
        ys, fn = _ffn_call(ys, state_ffn_conv, l, w, nb=8, t=dec_seq)
        pool_s.append(pn)
        conv_s.append(cn)
        ffn_s.append(fn)

    return (yp, ys, mk_out, mv_out, jnp.stack(pool_p), jnp.stack(conv_p),
            jnp.stack(ffn_p), jnp.stack(pool_s), jnp.stack(conv_s), jnp.stack(ffn_s))
```
